```python
import math
import jax
import jax.numpy as jnp
from jax import lax
import numpy as np

D_MODEL = 1024
BATCH = 8
SEQ = 4096
DEPTH = 2

N_A_LAYERS = DEPTH // 2
N_B_LAYERS = DEPTH - N_A_LAYERS
DN_ALPHA = (2.0 * DEPTH) ** 0.25
DN_BETA = (8.0 * DEPTH) ** -0.25
LN_EPS = 1e-5
HEAD_NORM_EPS = 1e-6
MIX_WIDTH = D_MODEL

MEM_LEN = 256
MEM_HEADS = 4
MEM_HEAD_DIM = D_MODEL // 16
MEM_WIDTH = MEM_HEADS * MEM_HEAD_DIM

GLA_HEADS = 4
GLA_DV = (MIX_WIDTH - MEM_WIDTH) // GLA_HEADS
GLA_DK = GLA_DV // 2
GLA_GATE_RANK = 16
GLA_TAU = 16.0
GLA_CHUNK = 64
GLA_QK_WIDTH = GLA_HEADS * GLA_DK
GLA_V_WIDTH = GLA_HEADS * GLA_DV

DIL_PAIRS = ((128, 1), (512, 4), (2048, 16))
DIL_GROUPS = len(DIL_PAIRS)
DIL_SLOTS = 6
DIL_HEAD_DIM = (MIX_WIDTH - MEM_WIDTH) // DIL_SLOTS
DIL_BLOCK = 128
DIL_Q_WIDTH = DIL_GROUPS * DIL_SLOTS * DIL_HEAD_DIM
DIL_KV_WIDTH = DIL_SLOTS * DIL_HEAD_DIM

PEER_N_KEYS = 128
PEER_EXPERTS = PEER_N_KEYS * PEER_N_KEYS
PEER_HEADS = 8
PEER_TOPK = 16
PEER_QDIM = 256
PEER_HALF = PEER_QDIM // 2
PEER_TOKEN_BLOCK = 128

A_Q0, A_Q1 = 0, GLA_QK_WIDTH
A_K0, A_K1 = A_Q1, A_Q1 + GLA_QK_WIDTH
A_V0, A_V1 = A_K1, A_K1 + GLA_V_WIDTH
A_R0, A_R1 = A_V1, A_V1 + GLA_V_WIDTH
A_G0, A_G1 = A_R1, A_R1 + GLA_GATE_RANK
A_M0, A_M1 = A_G1, A_G1 + MEM_WIDTH
A_IN_WIDTH = A_M1
B_Q0, B_Q1 = 0, DIL_Q_WIDTH
B_M0, B_M1 = B_Q1, B_Q1 + MEM_WIDTH
B_IN_WIDTH = B_M1

kernel_name = "hybrid_gla_dilated_peer_deepnorm"


def _layer_norm(x, g, b):
    xf = x.astype(jnp.float32)
    mu = jnp.mean(xf, axis=-1, keepdims=True)
    var = jnp.mean(jnp.square(xf - mu), axis=-1, keepdims=True)
    y = (xf - mu) * lax.rsqrt(var + LN_EPS)
    return (y * g.astype(jnp.float32) + b.astype(jnp.float32)).astype(x.dtype)


def _gla(q, k, v, log_a):
    B, S, H, K = q.shape
    nc = S // GLA_CHUNK

    def chunks(t):
        return t.astype(jnp.float32).reshape(B, nc, GLA_CHUNK, H, -1).transpose(1, 0, 3, 2, 4)

    q_c = chunks(q) * (K ** -0.5)
    k_c, v_c, g_c = chunks(k), chunks(v), chunks(log_a)
    b = jnp.cumsum(g_c, axis=3)
    b_last = b[:, :, :, -1:, :]
    q_t = q_c * jnp.exp(b)
    k_t = k_c * jnp.exp(-b)
    k_end = k_c * jnp.exp(b_last - b)
    causal = jnp.tril(jnp.ones((GLA_CHUNK, GLA_CHUNK), dtype=bool))
    attn = jnp.where(causal, jnp.einsum('nbhck,nbhsk->nbhcs', q_t, k_t), 0.0)
    o_intra = jnp.einsum('nbhcs,nbhsv->nbhcv', attn, v_c)

    def step(state, inp):
        qt, ke, vc, dec = inp
        o = jnp.einsum('bhck,bhkv->bhcv', qt, state)
        state = dec[..., None] * state + jnp.einsum('bhck,bhcv->bhkv', ke, vc)
        return state, o

    s0 = jnp.zeros((B, H, K, v.shape[-1]), jnp.float32)
    _, o_inter = lax.scan(step, s0, (q_t, k_end, v_c, jnp.exp(b_last[:, :, :, 0, :])))
    return (o_intra + o_inter).transpose(1, 0, 3, 2, 4).reshape(B, S, H, -1)


def _dilated_group(q, k, v, window, dilation):
    B, S, H, E = q.shape
    L = S // dilation
    nb = -(-L // DIL_BLOCK)
    Lp = nb * DIL_BLOCK
    span = window // dilation

    def sub(t):
        return t.reshape(B, L, dilation, H, E).transpose(0, 2, 3, 1, 4)

    qs = jnp.pad(sub(q), ((0, 0), (0, 0), (0, 0), (0, Lp - L), (0, 0)))
    qs = qs.reshape(B, dilation, H, nb, DIL_BLOCK, E)

    def kv_blocks(t):
        tp = jnp.pad(sub(t), ((0, 0), (0, 0), (0, 0), (DIL_BLOCK, Lp - L), (0, 0)))
        prev = tp[:, :, :, :Lp].reshape(B, dilation, H, nb, DIL_BLOCK, E)
        cur = tp[:, :, :, DIL_BLOCK:].reshape(B, dilation, H, nb, DIL_BLOCK, E)
        return jnp.concatenate([prev, cur], axis=4)

    kb, vb = kv_blocks(k), kv_blocks(v)
    s = jnp.einsum('bdhnqe,bdhnke->bdhnqk', qs, kb,
                   preferred_element_type=jnp.float32) * (E ** -0.5)
    qi = jnp.arange(DIL_BLOCK)[:, None]
    kj = jnp.arange(2 * DIL_BLOCK)[None, :]
    rel = DIL_BLOCK + qi - kj
    band = (rel >= 0) & (rel <= span)
    not_front = (jnp.arange(nb)[:, None, None] > 0) | (kj[None] >= DIL_BLOCK)
    mask = band[None] & not_front
    s = jnp.where(mask, s, -jnp.inf)
    lse = jax.nn.logsumexp(s, axis=-1)
    p = jnp.exp(s - lse[..., None])
    o = jnp.einsum('bdhnqk,bdhnke->bdhnqe', p.astype(vb.dtype), vb,
                   preferred_element_type=jnp.float32)
    o = o.reshape(B, dilation, H, Lp, E)[:, :, :, :L].transpose(0, 3, 1, 2, 4).reshape(B, S, H, E)
    lse = lse.reshape(B, dilation, H, Lp)[..., :L].transpose(0, 3, 1, 2).reshape(B, S, H)
    return o, lse


def _dilated_attention(q_all, k, v):
    outs, lses = [], []
    for g, (window, dilation) in enumerate(DIL_PAIRS):
        o, l = _dilated_group(q_all[:, :, g], k, v, window, dilation)
        outs.append(o)
        lses.append(l)
    wts = jax.nn.softmax(jnp.stack(lses, axis=0), axis=0)
    return jnp.einsum('gbsh,gbshe->bshe', wts, jnp.stack(outs, axis=0))


def _mem_attn(qm, km, vm):
    s = jnp.einsum('bshe,bmhe->bhsm', qm, km,
                   preferred_element_type=jnp.float32) * (qm.shape[-1] ** -0.5)
    p = jax.nn.softmax(s, axis=-1)
    return jnp.einsum('bhsm,bmhe->bshe', p.astype(vm.dtype), vm)


def _peer(x, w_q, sub_keys, u, v):
    B, S, D = x.shape
    T = B * S
    xt = x.reshape(T, D)
    q = (xt @ w_q).reshape(T, PEER_HEADS, 2, PEER_HALF)
    s = jnp.einsum('thpe,pne->thpn', q, sub_keys, preferred_element_type=jnp.float32)
    top_s, top_i = lax.top_k(s, PEER_TOPK)
    cand = (top_s[:, :, 0, :, None] + top_s[:, :, 1, None, :]).reshape(
        T, PEER_HEADS, PEER_TOPK * PEER_TOPK)
    best_s, best_j = lax.top_k(cand, PEER_TOPK)
    idx_a = jnp.take_along_axis(top_i[:, :, 0], best_j // PEER_TOPK, axis=-1)
    idx_b = jnp.take_along_axis(top_i[:, :, 1], best_j % PEER_TOPK, axis=-1)
    expert = idx_a * PEER_N_KEYS + idx_b
    gate = jax.nn.softmax(best_s, axis=-1)
    nblk = T // PEER_TOKEN_BLOCK

    def block(args):
        xb, eb, gb = args
        h = jnp.einsum('td,thkd->thk', xb, u[eb], preferred_element_type=jnp.float32)
        a = (jax.nn.gelu(h, approximate=False) * gb).astype(xb.dtype)
        return jnp.einsum('thk,thkd->td', a, v[eb])

    out = lax.map(block, (xt.reshape(nblk, PEER_TOKEN_BLOCK, D),
                          expert.reshape(nblk, PEER_TOKEN_BLOCK, PEER_HEADS, PEER_TOPK),
                          gate.reshape(nblk, PEER_TOKEN_BLOCK, PEER_HEADS, PEER_TOPK)))
    return out.reshape(B, S, D).astype(x.dtype)


def setup_inputs(seed: int = 0) -> dict:
    key = jax.random.key(seed)
    ks = jax.random.split(key, 20)
    n = jax.random.normal
    D = D_MODEL
    return {
        'x': n(ks[0], (BATCH, SEQ, D), jnp.float32),
        'mem': n(ks[1], (BATCH, MEM_LEN, D), jnp.float32),
        'a_w_in': n(ks[2], (N_A_LAYERS, D, A_IN_WIDTH), jnp.float32) * D ** -0.5,
        'a_w_gate2': n(ks[3], (N_A_LAYERS, GLA_GATE_RANK, GLA_QK_WIDTH), jnp.float32) * GLA_GATE_RANK ** -0.5,
        'a_b_gate': 0.02 * n(ks[4], (N_A_LAYERS, GLA_QK_WIDTH), jnp.float32),
        'a_norm_g': 1.0 + 0.02 * n(ks[5], (N_A_LAYERS, GLA_V_WIDTH), jnp.float32),
        'b_w_in': n(ks[6], (N_B_LAYERS, D, B_IN_WIDTH), jnp.float32) * D ** -0.5,
        'shared_w_kv': n(ks[7], (D, 2 * DIL_KV_WIDTH), jnp.float32) * D ** -0.5,
        'w_mem_kv': n(ks[8], (DEPTH, D, 2 * MEM_WIDTH), jnp.float32) * D ** -0.5,
        'w_out': n(ks[9], (DEPTH, MIX_WIDTH, D), jnp.float32) * (MIX_WIDTH ** -0.5 * DN_BETA),
        'ln_mix_g': 1.0 + 0.02 * n(ks[10], (DEPTH, D), jnp.float32),
        'ln_mix_b': 0.02 * n(ks[11], (DEPTH, D), jnp.float32),
        'ln_ffn_g': 1.0 + 0.02 * n(ks[12], (DEPTH, D), jnp.float32),
        'ln_ffn_b': 0.02 * n(ks[13], (DEPTH, D), jnp.float32),
        'peer_w_q': n(ks[14], (DEPTH, D, PEER_HEADS * PEER_QDIM), jnp.float32) * D ** -0.5,
        'peer_sub_keys': n(ks[15], (DEPTH, 2, PEER_N_KEYS, PEER_HALF), jnp.float32) * PEER_HALF ** -0.5,
        'peer_u': n(ks[16], (DEPTH, PEER_EXPERTS, D), jnp.float32) * D ** -0.5,
        'peer_v': n(ks[17], (DEPTH, PEER_EXPERTS, D), jnp.float32) * (DN_BETA * PEER_HEADS ** -0.5),
    }


def reference(x, mem, a_w_in, a_w_gate2, a_b_gate, a_norm_g, b_w_in, shared_w_kv,
              w_mem_kv, w_out, ln_mix_g, ln_mix_b, ln_ffn_g, ln_ffn_b,
              peer_w_q, peer_sub_keys, peer_u, peer_v):
    B, S, _ = x.shape
    shared_k = None
    shared_v = None
    for l in range(DEPTH):
        mkv = (mem @ w_mem_kv[l]).reshape(B, MEM_LEN, 2, MEM_HEADS, MEM_HEAD_DIM)
        if l < N_A_LAYERS:
            h = x @ a_w_in[l]
            q = h[..., A_Q0:A_Q1].reshape(B, S, GLA_HEADS, GLA_DK)
            k = h[..., A_K0:A_K1].reshape(B, S, GLA_HEADS, GLA_DK)
            v = h[..., A_V0:A_V1].reshape(B, S, GLA_HEADS, GLA_DV)
            r = h[..., A_R0:A_R1]
            g_pre = (h[..., A_G0:A_G1] @ a_w_gate2[l] + a_b_gate[l]).astype(jnp.float32)
            log_a = (jax.nn.log_sigmoid(g_pre) / GLA_TAU).reshape(B, S, GLA_HEADS, GLA_DK)
            o = _gla(q, k, v, log_a)
            o = o * lax.rsqrt(jnp.mean(jnp.square(o), axis=-1, keepdims=True) + HEAD_NORM_EPS)
            o = o.reshape(B, S, GLA_V_WIDTH) * a_norm_g[l].astype(jnp.float32)
            mix = (jax.nn.silu(r.astype(jnp.float32)) * o).astype(x.dtype)
            qm = h[..., A_M0:A_M1]
        else:
            h = x @ b_w_in[l - N_A_LAYERS]
            qd = h[..., B_Q0:B_Q1].reshape(B, S, DIL_GROUPS, DIL_SLOTS, DIL_HEAD_DIM)
            mix = _dilated_attention(qd, shared_k, shared_v).reshape(B, S, DIL_KV_WIDTH).astype(x.dtype)
            qm = h[..., B_M0:B_M1]
        mo = _mem_attn(qm.reshape(B, S, MEM_HEADS, MEM_HEAD_DIM),
                       mkv[:, :, 0], mkv[:, :, 1]).reshape(B, S, MEM_WIDTH).astype(x.dtype)
        y = jnp.concatenate([mix, mo], axis=-1) @ w_out[l]
        x = _layer_norm(DN_ALPHA * x + y, ln_mix_g[l], ln_mix_b[l])
        f = _peer(x, peer_w_q[l], peer_sub_keys[l], peer_u[l], peer_v[l])
        x = _layer_norm(DN_ALPHA * x + f, ln_ffn_g[l], ln_ffn_b[l])
        if l == N_A_LAYERS - 1:
            kv = (x @ shared_w_kv).reshape(B, S, 2, DIL_SLOTS, DIL_HEAD_DIM)
            shared_k = kv[:, :, 0]
            shared_v = kv[:, :, 1]
    return x
```

```python
import functools

import jax
import jax.numpy as jnp
from jax import lax
from jax.experimental import pallas as pl
from jax.experimental.pallas import tpu as pltpu

D_MODEL = 1024
DEPTH = 2
N_A_LAYERS = 1
DN_ALPHA = (2.0 * DEPTH) ** 0.25
LN_EPS = 1e-5
HEAD_NORM_EPS = 1e-6

MEM_LEN = 256
MEM_HEADS = 4
MEM_HEAD_DIM = 64
MEM_HEAD_SHIFT = 6
MEM_WIDTH = 256

GLA_HEADS = 4
GLA_DK = 96
GLA_DV = 192
GLA_DK_PAD = 128
GLA_DV_PAD = 256
GLA_GATE_RANK = 16
GLA_TAU = 16.0
GLA_CHUNK = 64
GLA_CHUNK_SHIFT = 6

DIL_PAIRS = ((128, 1), (512, 4), (2048, 16))
DIL_SLOTS = 6
DIL_HEAD_DIM = 128
DIL_BLOCK = 128
DIL_KV_WIDTH = DIL_SLOTS * DIL_HEAD_DIM

PEER_N_KEYS = 128
PEER_HEADS = 8
PEER_TOPK = 16
PEER_HALF = 128
PEER_SEL = PEER_HEADS * PEER_TOPK
PEER_ROW_WORDS = 4
PEER_CHUNKS = D_MODEL // 128
PEER_CHUNK_SHIFT = 3
PEER_TOPK_SHIFT = 4

LANES = 128
VMEM_LIMIT_BYTES = 56 * 1024 * 1024

BF16 = jnp.bfloat16
F32 = jnp.float32


def _params(n_grid_dims):
    return pltpu.CompilerParams(dimension_semantics=("arbitrary",) * n_grid_dims,
                                vmem_limit_bytes=VMEM_LIMIT_BYTES)


def _dot(a, b):
    return jnp.dot(a, b, preferred_element_type=F32)


def _dot_nt(a, b):
    return lax.dot_general(a, b, (((1,), (1,)), ((), ())), preferred_element_type=F32)


def _dot_tn(a, b):
    return lax.dot_general(a, b, (((0,), (0,)), ((), ())), preferred_element_type=F32)


def _split3(v):
    hi = v.astype(BF16)
    r1 = v - hi.astype(F32)
    mid = r1.astype(BF16)
    lo = (r1 - mid.astype(F32)).astype(BF16)
    return hi, mid, lo


def _layer_norm(z, g, b):
    mu = jnp.mean(z, axis=-1, keepdims=True)
    zc = z - mu
    var = jnp.mean(zc * zc, axis=-1, keepdims=True)
    return zc * lax.rsqrt(var + LN_EPS) * g + b


def _mm_kernel(x_ref, w_ref, *o_refs, splits):
    y = _dot(x_ref[...].astype(BF16), w_ref[...])
    off = 0
    for o_ref, width in zip(o_refs, splits):
        o_ref[...] = y[:, off:off + width].astype(o_ref.dtype)
        off += width


def _matmul(x, w, splits, dtypes, *, tm=256):
    t, k = x.shape
    n = w.shape[1]
    assert sum(splits) == n and t % tm == 0
    return pl.pallas_call(
        functools.partial(_mm_kernel, splits=tuple(splits)),
        grid=(t // tm,),
        in_specs=[pl.BlockSpec((tm, k), lambda i: (i, 0)),
                  pl.BlockSpec((k, n), lambda i: (0, 0))],
        out_specs=[pl.BlockSpec((tm, s), lambda i: (i, 0)) for s in splits],
        out_shape=[jax.ShapeDtypeStruct((t, s), d) for s, d in zip(splits, dtypes)],
        compiler_params=_params(1),
        name="matmul",
    )(x, w)


def _gla_kernel(q_ref, k_ref, v_ref, r_ref, g_ref, wg_ref, bg_ref, ng_ref, o_ref, state_ref, *, rows):
    nchunk = rows // GLA_CHUNK

    @pl.when(pl.program_id(1) == 0)
    def _():
        state_ref[...] = jnp.zeros_like(state_ref)

    g_pre = _dot(g_ref[...].astype(BF16), wg_ref[...]) + bg_ref[...]
    log_a = (jnp.minimum(g_pre, 0.0) - jnp.log1p(jnp.exp(-jnp.abs(g_pre)))) / GLA_TAU

    ri = lax.broadcasted_iota(jnp.int32, (rows, rows), 0)
    ci = lax.broadcasted_iota(jnp.int32, (rows, rows), 1)
    same_chunk_causal = (ci <= ri) & ((ri >> GLA_CHUNK_SHIFT) == (ci >> GLA_CHUNK_SHIFT))
    tril = same_chunk_causal.astype(BF16)
    b = jnp.zeros_like(log_a)
    for part in _split3(log_a):
        b = b + _dot(tril, part)
    b_last = jnp.concatenate(
        [jnp.broadcast_to(b[(c + 1) * GLA_CHUNK - 1:(c + 1) * GLA_CHUNK, :], (GLA_CHUNK, b.shape[1]))
         for c in range(nchunk)], axis=0)

    q = q_ref[...] * (GLA_DK ** -0.5)
    k = k_ref[...]
    q_t = (q * jnp.exp(b)).astype(BF16)
    k_t = (k * jnp.exp(-b)).astype(BF16)
    k_end = (k * jnp.exp(b_last - b)).astype(BF16)
    decay = jnp.exp(b_last)
    v = v_ref[...].astype(BF16)
    r = r_ref[...]
    ng = ng_ref[...]

    for h in range(GLA_HEADS):
        ks = slice(h * GLA_DK_PAD, (h + 1) * GLA_DK_PAD)
        vs = slice(h * GLA_DV_PAD, (h + 1) * GLA_DV_PAD)
        attn = jnp.where(same_chunk_causal, _dot_nt(q_t[:, ks], k_t[:, ks]), 0.0).astype(BF16)
        o_intra = _dot(attn, v[:, vs])
        st = state_ref[h]
        o_inter = []
        for c in range(nchunk):
            rs = slice(c * GLA_CHUNK, (c + 1) * GLA_CHUNK)
            o_inter.append(_dot_nt(q_t[rs, ks], st.astype(BF16)))
            st = st * decay[c * GLA_CHUNK:c * GLA_CHUNK + 1, ks] + _dot_tn(v[rs, vs], k_end[rs, ks])
        state_ref[h] = st
        o = o_intra + jnp.concatenate(o_inter, axis=0)
        ms = jnp.sum(o * o, axis=-1, keepdims=True) / GLA_DV
        o = o * lax.rsqrt(ms + HEAD_NORM_EPS) * ng[:, vs]
        rh = r[:, vs]
        o_ref[:, vs] = (rh * jax.nn.sigmoid(rh) * o).astype(o_ref.dtype)


def _gla(q, k, v, r, g, wg, bg, ng, *, batch, seq, rows=256):
    nblk = seq // rows
    row_map = lambda b, i: (b * nblk + i, 0)
    const = lambda b, i: (0, 0)
    qk_w, v_w = GLA_HEADS * GLA_DK_PAD, GLA_HEADS * GLA_DV_PAD
    return pl.pallas_call(
        functools.partial(_gla_kernel, rows=rows),
        grid=(batch, nblk),
        in_specs=[pl.BlockSpec((rows, qk_w), row_map), pl.BlockSpec((rows, qk_w), row_map),
                  pl.BlockSpec((rows, v_w), row_map), pl.BlockSpec((rows, v_w), row_map),
                  pl.BlockSpec((rows, LANES), row_map),
                  pl.BlockSpec((LANES, qk_w), const), pl.BlockSpec((1, qk_w), const),
                  pl.BlockSpec((1, v_w), const)],
        out_specs=pl.BlockSpec((rows, v_w), row_map),
        out_shape=jax.ShapeDtypeStruct((batch * seq, v_w), BF16),
        scratch_shapes=[pltpu.VMEM((GLA_HEADS, GLA_DV_PAD, GLA_DK_PAD), F32)],
        compiler_params=_params(2),
        name="gla",
    )(q, k, v, r, g, wg, bg, ng)


def _dilated_kernel(q_ref, kp_ref, kc_ref, vp_ref, vc_ref, o_ref, lse_ref, *, span):
    first_key = jnp.where(pl.program_id(2) == 0, DIL_BLOCK, 0)
    qi = lax.broadcasted_iota(jnp.int32, (DIL_BLOCK, 2 * DIL_BLOCK), 0)
    kj = lax.broadcasted_iota(jnp.int32, (DIL_BLOCK, 2 * DIL_BLOCK), 1)
    rel = DIL_BLOCK + qi - kj
    mask = (rel >= 0) & (rel <= span) & (kj >= first_key)
    scale = DIL_HEAD_DIM ** -0.5
    for h in range(DIL_SLOTS):
        hs = slice(h * DIL_HEAD_DIM, (h + 1) * DIL_HEAD_DIM)
        q = q_ref[:, hs].astype(BF16)
        kcat = jnp.concatenate([kp_ref[:, hs], kc_ref[:, hs]], axis=0).astype(BF16)
        vcat = jnp.concatenate([vp_ref[:, hs], vc_ref[:, hs]], axis=0).astype(BF16)
        s = jnp.where(mask, _dot_nt(q, kcat) * scale, -jnp.inf)
        m = jnp.max(s, axis=-1, keepdims=True)
        p = jnp.exp(s - m)
        l = jnp.sum(p, axis=-1, keepdims=True)
        o_ref[:, hs] = _dot(p.astype(BF16), vcat) / l
        lse_ref[:, hs] = jnp.broadcast_to(m + jnp.log(l), (DIL_BLOCK, DIL_HEAD_DIM))


def _dilated_group(q, k, v, *, batch, seq, window, dilation):
    sub_len = seq // dilation
    nb = sub_len // DIL_BLOCK
    w = DIL_KV_WIDTH
    view = lambda a: a.reshape(batch * sub_len, dilation * w)
    cur = lambda b, r, n: (b * nb + n, r)
    prev = lambda b, r, n: (b * nb + jnp.maximum(n - 1, 0), r)
    blk = (DIL_BLOCK, w)
    o, lse = pl.pallas_call(
        functools.partial(_dilated_kernel, span=window // dilation),
        grid=(batch, dilation, nb),
        in_specs=[pl.BlockSpec(blk, cur), pl.BlockSpec(blk, prev), pl.BlockSpec(blk, cur),
                  pl.BlockSpec(blk, prev), pl.BlockSpec(blk, cur)],
        out_specs=[pl.BlockSpec(blk, cur), pl.BlockSpec(blk, cur)],
        out_shape=[jax.ShapeDtypeStruct((batch * sub_len, dilation * w), F32)] * 2,
        compiler_params=_params(3),
        name="dilated_attention",
    )(view(q), view(k), view(k), view(v), view(v))
    return o.reshape(batch * seq, w), lse.reshape(batch * seq, w)


def _post_kernel(*refs, n_groups):
    x_ref, qm_ref, km_ref, vm_ref, wmix_ref, wmem_ref, g_ref, b_ref = refs[:8]
    mix_refs = refs[8:-1]
    o_ref = refs[-1]
    if n_groups == 0:
        mix = mix_refs[0][...]
    else:
        lses = [mix_refs[n_groups + g][...] for g in range(n_groups)]
        mx = functools.reduce(jnp.maximum, lses)
        ws = [jnp.exp(l - mx) for l in lses]
        den = functools.reduce(lambda a, c: a + c, ws)
        mix = functools.reduce(lambda a, c: a + c, [w * mix_refs[g][...] for g, w in enumerate(ws)]) / den
    qm = qm_ref[...]
    km = km_ref[...].astype(BF16)
    vm = vm_ref[...].astype(BF16)
    lane = lax.broadcasted_iota(jnp.int32, qm.shape, 1)
    mo = jnp.zeros(qm.shape, F32)
    for h in range(MEM_HEADS):
        head = (lane >> MEM_HEAD_SHIFT) == h
        s = _dot_nt(jnp.where(head, qm, 0.0).astype(BF16), km) * (MEM_HEAD_DIM ** -0.5)
        m = jnp.max(s, axis=-1, keepdims=True)
        p = jnp.exp(s - m)
        p = p / jnp.sum(p, axis=-1, keepdims=True)
        mo = mo + jnp.where(head, _dot(p.astype(BF16), vm), 0.0)
    y = _dot(mix.astype(BF16), wmix_ref[...]) + _dot(mo.astype(BF16), wmem_ref[...])
    o_ref[...] = _layer_norm(DN_ALPHA * x_ref[...] + y, g_ref[...], b_ref[...])


def _post_mixer(x, qm, mkv, w_mix, w_mem, g, b, mixes, *, seq, n_groups, tm=256):
    t = x.shape[0]
    per_batch = seq // tm
    row = lambda i: (i, 0)
    const = lambda i: (0, 0)
    mix_w = w_mix.shape[0]
    in_specs = [pl.BlockSpec((tm, D_MODEL), row), pl.BlockSpec((tm, MEM_WIDTH), row),
                pl.BlockSpec((MEM_LEN, MEM_WIDTH), lambda i: (i // per_batch, 0)),
                pl.BlockSpec((MEM_LEN, MEM_WIDTH), lambda i: (i // per_batch, 1)),
                pl.BlockSpec((mix_w, D_MODEL), const), pl.BlockSpec((MEM_WIDTH, D_MODEL), const),
                pl.BlockSpec((1, D_MODEL), const), pl.BlockSpec((1, D_MODEL), const)]
    in_specs += [pl.BlockSpec((tm, mix_w), row) for _ in mixes]
    return pl.pallas_call(
        functools.partial(_post_kernel, n_groups=n_groups),
        grid=(t // tm,),
        in_specs=in_specs,
        out_specs=pl.BlockSpec((tm, D_MODEL), row),
        out_shape=jax.ShapeDtypeStruct((t, D_MODEL), F32),
        compiler_params=_params(1),
        name="post_mixer",
    )(x, qm, mkv, mkv, w_mix, w_mem, g, b, *mixes)


def _top_rows(s, n_rows, val_ref, idx_ref):
    iota = lax.broadcasted_iota(jnp.int32, s.shape, 0).astype(F32)
    for r in range(PEER_TOPK):
        m = jnp.max(s, axis=0, keepdims=True)
        am = jnp.min(jnp.where(s == m, iota, float(n_rows)), axis=0, keepdims=True)
        val_ref[r:r + 1, :] = m
        idx_ref[r:r + 1, :] = am
        s = jnp.where(iota == am, -jnp.inf, s)


def _route_kernel(x_ref, wq_ref, keys_ref, idx_ref, gate_ref, q_ref, tv_ref, ti_ref, bv_ref, bi_ref,
                  e_ref, gt_ref):
    tm = x_ref.shape[0]
    q_ref[...] = _dot(x_ref[...].astype(BF16), wq_ref[...]).astype(BF16)
    for h in range(PEER_HEADS):
        for p in range(2):
            c0 = (h * 2 + p) * PEER_HALF
            s = _dot_nt(keys_ref[p], q_ref[:, c0:c0 + PEER_HALF])
            _top_rows(s, PEER_N_KEYS, tv_ref.at[p], ti_ref.at[p])
        v1 = tv_ref[1]
        cand = jnp.concatenate([tv_ref[0, a:a + 1, :] + v1 for a in range(PEER_TOPK)], axis=0)
        _top_rows(cand, PEER_TOPK * PEER_TOPK, bv_ref, bi_ref)
        best_s = bv_ref[...]
        best_j = bi_ref[...].astype(jnp.int32)
        ja = best_j >> PEER_TOPK_SHIFT
        jb = best_j & (PEER_TOPK - 1)
        ia = jnp.zeros(best_j.shape, F32)
        ib = jnp.zeros(best_j.shape, F32)
        for a in range(PEER_TOPK):
            ia = jnp.where(ja == a, ti_ref[0, a:a + 1, :], ia)
            ib = jnp.where(jb == a, ti_ref[1, a:a + 1, :], ib)
        pe = jnp.exp(best_s - best_s[0:1, :])
        rs = slice(h * PEER_TOPK, (h + 1) * PEER_TOPK)
        e_ref[rs, :] = (ia.astype(jnp.int32) * PEER_N_KEYS + ib.astype(jnp.int32)) * PEER_ROW_WORDS
        gt_ref[rs, :] = pe / jnp.sum(pe, axis=0, keepdims=True)
    idx_ref[...] = pltpu.bitcast(pltpu.bitcast(e_ref[...], F32).T, jnp.int32)
    gate_ref[...] = gt_ref[...].T


def _peer_route(x, wq, keys, *, tm=256):
    t = x.shape[0]
    nq = wq.shape[1]
    return pl.pallas_call(
        _route_kernel,
        grid=(t // tm,),
        in_specs=[pl.BlockSpec((tm, D_MODEL), lambda i: (i, 0)),
                  pl.BlockSpec((D_MODEL, nq), lambda i: (0, 0)),
                  pl.BlockSpec((2, PEER_N_KEYS, PEER_HALF), lambda i: (0, 0, 0))],
        out_specs=[pl.BlockSpec((tm, PEER_SEL), lambda i: (i, 0))] * 2,
        out_shape=[jax.ShapeDtypeStruct((t, PEER_SEL), jnp.int32),
                   jax.ShapeDtypeStruct((t, PEER_SEL), F32)],
        scratch_shapes=[pltpu.VMEM((tm, nq), BF16),
                        pltpu.VMEM((2, PEER_TOPK, tm), F32), pltpu.VMEM((2, PEER_TOPK, tm), F32),
                        pltpu.VMEM((PEER_TOPK, tm), F32), pltpu.VMEM((PEER_TOPK, tm), F32),
                        pltpu.VMEM((PEER_SEL, tm), jnp.int32), pltpu.VMEM((PEER_SEL, tm), F32)],
        compiler_params=_params(1),
        name="peer_route",
    )(x, wq, keys)


def _pack_table(w):
    e = w.shape[0]
    wb = w.astype(BF16).reshape(e, PEER_ROW_WORDS, 2, LANES).transpose(0, 1, 3, 2)
    return lax.bitcast_convert_type(wb, jnp.uint32).reshape(e * PEER_ROW_WORDS, LANES)


def _gather_rows(idx_ref, t, tbl_ref, w_ref):
    for j in range(PEER_SEL):
        i = pl.multiple_of(idx_ref[t, j], PEER_ROW_WORDS)
        w_ref[PEER_ROW_WORDS * j:PEER_ROW_WORDS * (j + 1), :] = tbl_ref[pl.ds(i, PEER_ROW_WORDS), :]


def _peer_u_kernel(idx_ref, x8_ref, gate_ref, tbl_ref, a_ref, w_ref, hb_ref, *, tb):
    width = PEER_SEL * PEER_CHUNKS
    lane = lax.broadcasted_iota(jnp.int32, (2 * PEER_CHUNKS, width), 1)
    row = lax.broadcasted_iota(jnp.int32, (2 * PEER_CHUNKS, width), 0)
    diag = (lane & (PEER_CHUNKS - 1)) == (row & (PEER_CHUNKS - 1))

    def body(t, carry):
        _gather_rows(idx_ref, t, tbl_ref, w_ref)
        wb = pltpu.bitcast(w_ref[...], BF16)
        x8 = x8_ref[pl.ds(pl.multiple_of(t * PEER_CHUNKS, PEER_CHUNKS), PEER_CHUNKS), :]
        xh = x8.astype(BF16)
        xl = (x8 - xh.astype(F32)).astype(BF16)
        r = _dot_nt(jnp.concatenate([xh, xl], axis=0), wb)
        hb_ref[pl.ds(t, 1), :] = jnp.sum(jnp.where(diag, r, 0.0), axis=0, keepdims=True)
        return carry

    lax.fori_loop(0, tb, body, 0)
    gi = lax.broadcasted_iota(jnp.int32, (width, PEER_SEL), 0)
    gj = lax.broadcasted_iota(jnp.int32, (width, PEER_SEL), 1)
    group_sum = ((gi >> PEER_CHUNK_SHIFT) == gj).astype(BF16)
    h = jnp.zeros((tb, PEER_SEL), F32)
    for part in _split3(hb_ref[...]):
        h = h + _dot(part, group_sum)
    a_ref[...] = 0.5 * h * (1.0 + lax.erf(h * (2.0 ** -0.5))) * gate_ref[...]


def _peer_hidden(idx, x8, gate, tbl, *, tb=64):
    t = idx.shape[0]
    return pl.pallas_call(
        functools.partial(_peer_u_kernel, tb=tb),
        grid=(t // tb,),
        in_specs=[pl.BlockSpec((tb, PEER_SEL), lambda i: (i, 0), memory_space=pltpu.SMEM),
                  pl.BlockSpec((tb * PEER_CHUNKS, LANES), lambda i: (i, 0)),
                  pl.BlockSpec((tb, PEER_SEL), lambda i: (i, 0)),
                  pl.BlockSpec(memory_space=pltpu.VMEM)],
        out_specs=pl.BlockSpec((tb, PEER_SEL), lambda i: (i, 0)),
        out_shape=jax.ShapeDtypeStruct((t, PEER_SEL), F32),
        scratch_shapes=[pltpu.VMEM((PEER_SEL * PEER_ROW_WORDS, LANES), jnp.uint32),
                        pltpu.VMEM((tb, PEER_SEL * PEER_CHUNKS), F32)],
        compiler_params=_params(1),
        name="peer_hidden",
    )(idx, x8, gate, tbl)


def _peer_v_kernel(idx_ref, a_ref, tbl_ref, f8_ref, w_ref, ah_ref, al_ref, *, tb):
    width = PEER_SEL * PEER_CHUNKS
    ei = lax.broadcasted_iota(jnp.int32, (PEER_SEL, width), 0)
    ej = lax.broadcasted_iota(jnp.int32, (PEER_SEL, width), 1)
    expand = (ei == (ej >> PEER_CHUNK_SHIFT)).astype(BF16)
    a = a_ref[...]
    ah = a.astype(BF16)
    al = (a - ah.astype(F32)).astype(BF16)
    ah_ref[...] = _dot(ah, expand)
    al_ref[...] = _dot(al, expand)
    lane = lax.broadcasted_iota(jnp.int32, (PEER_CHUNKS, width), 1)
    row = lax.broadcasted_iota(jnp.int32, (PEER_CHUNKS, width), 0)
    diag = (lane & (PEER_CHUNKS - 1)) == row

    def body(t, carry):
        _gather_rows(idx_ref, t, tbl_ref, w_ref)
        wb = pltpu.bitcast(w_ref[...], BF16)
        a8h = jnp.where(diag, ah_ref[pl.ds(t, 1), :], 0.0).astype(BF16)
        a8l = jnp.where(diag, al_ref[pl.ds(t, 1), :], 0.0).astype(BF16)
        r = _dot(jnp.concatenate([a8h, a8l], axis=0), wb)
        f8_ref[pl.ds(pl.multiple_of(t * PEER_CHUNKS, PEER_CHUNKS), PEER_CHUNKS), :] = (
            r[0:PEER_CHUNKS] + r[PEER_CHUNKS:2 * PEER_CHUNKS])
        return carry

    lax.fori_loop(0, tb, body, 0)


def _peer_output(idx, a, tbl, *, tb=64):
    t = idx.shape[0]
    return pl.pallas_call(
        functools.partial(_peer_v_kernel, tb=tb),
        grid=(t // tb,),
        in_specs=[pl.BlockSpec((tb, PEER_SEL), lambda i: (i, 0), memory_space=pltpu.SMEM),
                  pl.BlockSpec((tb, PEER_SEL), lambda i: (i, 0)),
                  pl.BlockSpec(memory_space=pltpu.VMEM)],
        out_specs=pl.BlockSpec((tb * PEER_CHUNKS, LANES), lambda i: (i, 0)),
        out_shape=jax.ShapeDtypeStruct((t * PEER_CHUNKS, LANES), F32),
        scratch_shapes=[pltpu.VMEM((PEER_SEL * PEER_ROW_WORDS, LANES), jnp.uint32),
                        pltpu.VMEM((tb, PEER_SEL * PEER_CHUNKS), F32),
                        pltpu.VMEM((tb, PEER_SEL * PEER_CHUNKS), F32)],
        compiler_params=_params(1),
        name="peer_output",
    )(idx, a, tbl)


def _add_norm_kernel(x_ref, f_ref, g_ref, b_ref, o_ref):
    o_ref[...] = _layer_norm(DN_ALPHA * x_ref[...] + f_ref[...], g_ref[...], b_ref[...])


def _add_norm(x, f, g, b, *, tm=512):
    t = x.shape[0]
    row = lambda i: (i, 0)
    const = lambda i: (0, 0)
    return pl.pallas_call(
        _add_norm_kernel,
        grid=(t // tm,),
        in_specs=[pl.BlockSpec((tm, D_MODEL), row), pl.BlockSpec((tm, D_MODEL), row),
                  pl.BlockSpec((1, D_MODEL), const), pl.BlockSpec((1, D_MODEL), const)],
        out_specs=pl.BlockSpec((tm, D_MODEL), row),
        out_shape=jax.ShapeDtypeStruct((t, D_MODEL), F32),
        compiler_params=_params(1),
        name="add_norm",
    )(x, f, g, b)


def _peer(x, w_q, sub_keys, u, v, g, b):
    t = x.shape[0]
    idx, gate = _peer_route(x, w_q.astype(BF16), sub_keys.astype(BF16))
    a = _peer_hidden(idx, x.reshape(t * PEER_CHUNKS, LANES), gate, _pack_table(u))
    f = _peer_output(idx, a, _pack_table(v)).reshape(t, D_MODEL)
    return _add_norm(x, f, g, b)


def _pad_heads(w, heads, dim, dim_pad, axis=-1):
    axis = axis % w.ndim
    shape = w.shape[:axis] + (heads, dim) + w.shape[axis + 1:]
    pad = [(0, 0)] * (w.ndim + 1)
    pad[axis + 1] = (0, dim_pad - dim)
    out = jnp.pad(w.reshape(shape), pad)
    return out.reshape(w.shape[:axis] + (heads * dim_pad,) + w.shape[axis + 1:])


def kernel(x, mem, a_w_in, a_w_gate2, a_b_gate, a_norm_g, b_w_in, shared_w_kv, w_mem_kv, w_out, ln_mix_g,
           ln_mix_b, ln_ffn_g, ln_ffn_b, peer_w_q, peer_sub_keys, peer_u, peer_v):
    batch, seq, d = x.shape
    t = batch * seq
    xt = x.reshape(t, d)
    mem2 = mem.reshape(batch * MEM_LEN, d)
    row = lambda a: a.reshape(1, -1)
    qk_w = GLA_HEADS * GLA_DK
    v_w = GLA_HEADS * GLA_DV
    kd = None
    vd = None
    for l in range(DEPTH):
        (mkv,) = _matmul(mem2, w_mem_kv[l].astype(BF16), [2 * MEM_WIDTH], [F32])
        w_mem = w_out[l, v_w:].astype(BF16)
        if l < N_A_LAYERS:
            w = a_w_in[l]
            q0, k0, v0, r0 = 0, qk_w, 2 * qk_w, 2 * qk_w + v_w
            g0 = r0 + v_w
            m0 = g0 + GLA_GATE_RANK
            w_in = jnp.concatenate([
                _pad_heads(w[:, q0:k0], GLA_HEADS, GLA_DK, GLA_DK_PAD),
                _pad_heads(w[:, k0:v0], GLA_HEADS, GLA_DK, GLA_DK_PAD),
                _pad_heads(w[:, v0:r0], GLA_HEADS, GLA_DV, GLA_DV_PAD),
                _pad_heads(w[:, r0:g0], GLA_HEADS, GLA_DV, GLA_DV_PAD),
                w[:, m0:m0 + MEM_WIDTH],
                jnp.pad(w[:, g0:m0], ((0, 0), (0, LANES - GLA_GATE_RANK)))], axis=1).astype(BF16)
            qkw, vw = GLA_HEADS * GLA_DK_PAD, GLA_HEADS * GLA_DV_PAD
            q, k, v, r, qm, gin = _matmul(xt, w_in, [qkw, qkw, vw, vw, MEM_WIDTH, LANES], [F32] * 6)
            wg = jnp.pad(_pad_heads(a_w_gate2[l], GLA_HEADS, GLA_DK, GLA_DK_PAD),
                         ((0, LANES - GLA_GATE_RANK), (0, 0))).astype(BF16)
            bg = row(_pad_heads(a_b_gate[l], GLA_HEADS, GLA_DK, GLA_DK_PAD))
            ng = row(_pad_heads(a_norm_g[l], GLA_HEADS, GLA_DV, GLA_DV_PAD))
            mix = _gla(q, k, v, r, gin, wg, bg, ng, batch=batch, seq=seq)
            w_mix = _pad_heads(w_out[l, :v_w], GLA_HEADS, GLA_DV, GLA_DV_PAD, axis=0).astype(BF16)
            mixes, n_groups = [mix], 0
        else:
            w_in = b_w_in[l - N_A_LAYERS].astype(BF16)
            n_g = len(DIL_PAIRS)
            outs = _matmul(xt, w_in, [DIL_KV_WIDTH] * n_g + [MEM_WIDTH], [F32] * (n_g + 1))
            qm = outs[n_g]
            res = [_dilated_group(outs[g], kd, vd, batch=batch, seq=seq, window=wd, dilation=dl)
                   for g, (wd, dl) in enumerate(DIL_PAIRS)]
            mixes = [o for o, _ in res] + [s for _, s in res]
            n_groups = n_g
            w_mix = w_out[l, :DIL_KV_WIDTH].astype(BF16)
        xt = _post_mixer(xt, qm, mkv, w_mix, w_mem, row(ln_mix_g[l]), row(ln_mix_b[l]), mixes,
                         seq=seq, n_groups=n_groups)
        xt = _peer(xt, peer_w_q[l], peer_sub_keys[l], peer_u[l], peer_v[l], row(ln_ffn_g[l]), row(ln_ffn_b[l]))
        if l == N_A_LAYERS - 1:
            kd, vd = _matmul(xt, shared_w_kv.astype(BF16), [DIL_KV_WIDTH] * 2, [F32] * 2)
    return xt.reshape(batch, seq, d)
```

```python
import functools

import jax
import jax.numpy as jnp
from jax import lax
from jax.experimental import pallas as pl
from jax.experimental.pallas import tpu as pltpu

D_MODEL = 1024
DEPTH = 2
N_A_LAYERS = 1
DN_ALPHA = (2.0 * DEPTH) ** 0.25
LN_EPS = 1e-5
HEAD_NORM_EPS = 1e-6

MEM_LEN = 256
MEM_HEADS = 4
MEM_HEAD_DIM = 64
MEM_HEAD_SHIFT = 6
MEM_WIDTH = 256

GLA_HEADS = 4
GLA_DK = 96
GLA_DV = 192
GLA_DK_PAD = 128
GLA_DV_PAD = 256
GLA_GATE_RANK = 16
GLA_TAU = 16.0
GLA_CHUNK = 64
GLA_CHUNK_SHIFT = 6

DIL_PAIRS = ((128, 1), (512, 4), (2048, 16))
DIL_SLOTS = 6
DIL_HEAD_DIM = 128
DIL_BLOCK = 128
DIL_KV_WIDTH = DIL_SLOTS * DIL_HEAD_DIM

PEER_N_KEYS = 128
PEER_HEADS = 8
PEER_TOPK = 16
PEER_HALF = 128
PEER_SEL = PEER_HEADS * PEER_TOPK
PEER_ROW_WORDS = 4
PEER_CHUNKS = D_MODEL // 128
PEER_CHUNK_SHIFT = 3
PEER_STAGES = 4
PEER_HIDDEN_BODY_TOKENS = 4
PEER_OUTPUT_BODY_TOKENS = 16
PEER_TOPK_SHIFT = 4

LANES = 128
VMEM_LIMIT_BYTES = 56 * 1024 * 1024

BF16 = jnp.bfloat16
F32 = jnp.float32


def _params(n_grid_dims):
    return pltpu.CompilerParams(dimension_semantics=("arbitrary",) * n_grid_dims,
                                vmem_limit_bytes=VMEM_LIMIT_BYTES)


def _dot(a, b):
    return jnp.dot(a, b, preferred_element_type=F32)


def _dot_nt(a, b):
    return lax.dot_general(a, b, (((1,), (1,)), ((), ())), preferred_element_type=F32)


def _dot_tn(a, b):
    return lax.dot_general(a, b, (((0,), (0,)), ((), ())), preferred_element_type=F32)


def _split3(v):
    hi = v.astype(BF16)
    r1 = v - hi.astype(F32)
    mid = r1.astype(BF16)
    lo = (r1 - mid.astype(F32)).astype(BF16)
    return hi, mid, lo


def _layer_norm(z, g, b):
    mu = jnp.mean(z, axis=-1, keepdims=True)
    zc = z - mu
    var = jnp.mean(zc * zc, axis=-1, keepdims=True)
    return zc * lax.rsqrt(var + LN_EPS) * g + b


def _mm_kernel(x_ref, w_ref, *o_refs, splits):
    y = _dot(x_ref[...].astype(BF16), w_ref[...])
    off = 0
    for o_ref, width in zip(o_refs, splits):
        o_ref[...] = y[:, off:off + width].astype(o_ref.dtype)
        off += width


def _matmul(x, w, splits, dtypes, *, tm=256):
    t, k = x.shape
    n = w.shape[1]
    assert sum(splits) == n and t % tm == 0
    return pl.pallas_call(
        functools.partial(_mm_kernel, splits=tuple(splits)),
        grid=(t // tm,),
        in_specs=[pl.BlockSpec((tm, k), lambda i: (i, 0)),
                  pl.BlockSpec((k, n), lambda i: (0, 0))],
        out_specs=[pl.BlockSpec((tm, s), lambda i: (i, 0)) for s in splits],
        out_shape=[jax.ShapeDtypeStruct((t, s), d) for s, d in zip(splits, dtypes)],
        compiler_params=_params(1),
        name="matmul",
    )(x, w)


def _gla_kernel(q_ref, k_ref, v_ref, r_ref, g_ref, wg_ref, bg_ref, ng_ref, o_ref, state_ref, *, rows):
    nchunk = rows // GLA_CHUNK

    @pl.when(pl.program_id(1) == 0)
    def _():
        state_ref[...] = jnp.zeros_like(state_ref)

    g_pre = _dot(g_ref[...].astype(BF16), wg_ref[...]) + bg_ref[...]
    log_a = (jnp.minimum(g_pre, 0.0) - jnp.log1p(jnp.exp(-jnp.abs(g_pre)))) / GLA_TAU

    ri = lax.broadcasted_iota(jnp.int32, (rows, rows), 0)
    ci = lax.broadcasted_iota(jnp.int32, (rows, rows), 1)
    same_chunk_causal = (ci <= ri) & ((ri >> GLA_CHUNK_SHIFT) == (ci >> GLA_CHUNK_SHIFT))
    tril = same_chunk_causal.astype(BF16)
    b = jnp.zeros_like(log_a)
    for part in _split3(log_a):
        b = b + _dot(tril, part)
    b_last = jnp.concatenate(
        [jnp.broadcast_to(b[(c + 1) * GLA_CHUNK - 1:(c + 1) * GLA_CHUNK, :], (GLA_CHUNK, b.shape[1]))
         for c in range(nchunk)], axis=0)

    q = q_ref[...] * (GLA_DK ** -0.5)
    k = k_ref[...]
    q_t = (q * jnp.exp(b)).astype(BF16)
    k_t = (k * jnp.exp(-b)).astype(BF16)
    k_end = (k * jnp.exp(b_last - b)).astype(BF16)
    decay = jnp.exp(b_last)
    v = v_ref[...].astype(BF16)
    r = r_ref[...]
    ng = ng_ref[...]

    for h in range(GLA_HEADS):
        ks = slice(h * GLA_DK_PAD, (h + 1) * GLA_DK_PAD)
        vs = slice(h * GLA_DV_PAD, (h + 1) * GLA_DV_PAD)
        attn = jnp.where(same_chunk_causal, _dot_nt(q_t[:, ks], k_t[:, ks]), 0.0).astype(BF16)
        o_intra = _dot(attn, v[:, vs])
        st = state_ref[h]
        o_inter = []
        for c in range(nchunk):
            rs = slice(c * GLA_CHUNK, (c + 1) * GLA_CHUNK)
            o_inter.append(_dot_nt(q_t[rs, ks], st.astype(BF16)))
            st = st * decay[c * GLA_CHUNK:c * GLA_CHUNK + 1, ks] + _dot_tn(v[rs, vs], k_end[rs, ks])
        state_ref[h] = st
        o = o_intra + jnp.concatenate(o_inter, axis=0)
        ms = jnp.sum(o * o, axis=-1, keepdims=True) / GLA_DV
        o = o * lax.rsqrt(ms + HEAD_NORM_EPS) * ng[:, vs]
        rh = r[:, vs]
        o_ref[:, vs] = (rh * jax.nn.sigmoid(rh) * o).astype(o_ref.dtype)


def _gla(q, k, v, r, g, wg, bg, ng, *, batch, seq, rows=256):
    nblk = seq // rows
    row_map = lambda b, i: (b * nblk + i, 0)
    const = lambda b, i: (0, 0)
    qk_w, v_w = GLA_HEADS * GLA_DK_PAD, GLA_HEADS * GLA_DV_PAD
    return pl.pallas_call(
        functools.partial(_gla_kernel, rows=rows),
        grid=(batch, nblk),
        in_specs=[pl.BlockSpec((rows, qk_w), row_map), pl.BlockSpec((rows, qk_w), row_map),
                  pl.BlockSpec((rows, v_w), row_map), pl.BlockSpec((rows, v_w), row_map),
                  pl.BlockSpec((rows, LANES), row_map),
                  pl.BlockSpec((LANES, qk_w), const), pl.BlockSpec((1, qk_w), const),
                  pl.BlockSpec((1, v_w), const)],
        out_specs=pl.BlockSpec((rows, v_w), row_map),
        out_shape=jax.ShapeDtypeStruct((batch * seq, v_w), BF16),
        scratch_shapes=[pltpu.VMEM((GLA_HEADS, GLA_DV_PAD, GLA_DK_PAD), F32)],
        compiler_params=_params(2),
        name="gla",
    )(q, k, v, r, g, wg, bg, ng)


def _dilated_kernel(q_ref, kp_ref, kc_ref, vp_ref, vc_ref, o_ref, lse_ref, *, span):
    first_key = jnp.where(pl.program_id(2) == 0, DIL_BLOCK, 0)
    qi = lax.broadcasted_iota(jnp.int32, (DIL_BLOCK, 2 * DIL_BLOCK), 0)
    kj = lax.broadcasted_iota(jnp.int32, (DIL_BLOCK, 2 * DIL_BLOCK), 1)
    rel = DIL_BLOCK + qi - kj
    mask = (rel >= 0) & (rel <= span) & (kj >= first_key)
    scale = DIL_HEAD_DIM ** -0.5
    for h in range(DIL_SLOTS):
        hs = slice(h * DIL_HEAD_DIM, (h + 1) * DIL_HEAD_DIM)
        q = q_ref[:, hs].astype(BF16)
        kcat = jnp.concatenate([kp_ref[:, hs], kc_ref[:, hs]], axis=0).astype(BF16)
        vcat = jnp.concatenate([vp_ref[:, hs], vc_ref[:, hs]], axis=0).astype(BF16)
        s = jnp.where(mask, _dot_nt(q, kcat) * scale, -jnp.inf)
        m = jnp.max(s, axis=-1, keepdims=True)
        p = jnp.exp(s - m)
        l = jnp.sum(p, axis=-1, keepdims=True)
        o_ref[:, hs] = _dot(p.astype(BF16), vcat) / l
        lse_ref[:, hs] = jnp.broadcast_to(m + jnp.log(l), (DIL_BLOCK, DIL_HEAD_DIM))


def _dilated_group(q, k, v, *, batch, seq, window, dilation):
    sub_len = seq // dilation
    nb = sub_len // DIL_BLOCK
    w = DIL_KV_WIDTH
    view = lambda a: a.reshape(batch * sub_len, dilation * w)
    cur = lambda b, r, n: (b * nb + n, r)
    prev = lambda b, r, n: (b * nb + jnp.maximum(n - 1, 0), r)
    blk = (DIL_BLOCK, w)
    o, lse = pl.pallas_call(
        functools.partial(_dilated_kernel, span=window // dilation),
        grid=(batch, dilation, nb),
        in_specs=[pl.BlockSpec(blk, cur), pl.BlockSpec(blk, prev), pl.BlockSpec(blk, cur),
                  pl.BlockSpec(blk, prev), pl.BlockSpec(blk, cur)],
        out_specs=[pl.BlockSpec(blk, cur), pl.BlockSpec(blk, cur)],
        out_shape=[jax.ShapeDtypeStruct((batch * sub_len, dilation * w), F32)] * 2,
        compiler_params=_params(3),
        name="dilated_attention",
    )(view(q), view(k), view(k), view(v), view(v))
    return o.reshape(batch * seq, w), lse.reshape(batch * seq, w)


def _post_kernel(*refs, n_groups):
    x_ref, qm_ref, km_ref, vm_ref, wmix_ref, wmem_ref, g_ref, b_ref = refs[:8]
    mix_refs = refs[8:-1]
    o_ref = refs[-1]
    if n_groups == 0:
        mix = mix_refs[0][...]
    else:
        lses = [mix_refs[n_groups + g][...] for g in range(n_groups)]
        mx = functools.reduce(jnp.maximum, lses)
        ws = [jnp.exp(l - mx) for l in lses]
        den = functools.reduce(lambda a, c: a + c, ws)
        mix = functools.reduce(lambda a, c: a + c, [w * mix_refs[g][...] for g, w in enumerate(ws)]) / den
    qm = qm_ref[...]
    km = km_ref[...].astype(BF16)
    vm = vm_ref[...].astype(BF16)
    lane = lax.broadcasted_iota(jnp.int32, qm.shape, 1)
    mo = jnp.zeros(qm.shape, F32)
    for h in range(MEM_HEADS):
        head = (lane >> MEM_HEAD_SHIFT) == h
        s = _dot_nt(jnp.where(head, qm, 0.0).astype(BF16), km) * (MEM_HEAD_DIM ** -0.5)
        m = jnp.max(s, axis=-1, keepdims=True)
        p = jnp.exp(s - m)
        p = p / jnp.sum(p, axis=-1, keepdims=True)
        mo = mo + jnp.where(head, _dot(p.astype(BF16), vm), 0.0)
    y = _dot(mix.astype(BF16), wmix_ref[...]) + _dot(mo.astype(BF16), wmem_ref[...])
    o_ref[...] = _layer_norm(DN_ALPHA * x_ref[...] + y, g_ref[...], b_ref[...])


def _post_mixer(x, qm, mkv, w_mix, w_mem, g, b, mixes, *, seq, n_groups, tm=256):
    t = x.shape[0]
    per_batch = seq // tm
    row = lambda i: (i, 0)
    const = lambda i: (0, 0)
    mix_w = w_mix.shape[0]
    in_specs = [pl.BlockSpec((tm, D_MODEL), row), pl.BlockSpec((tm, MEM_WIDTH), row),
                pl.BlockSpec((MEM_LEN, MEM_WIDTH), lambda i: (i // per_batch, 0)),
                pl.BlockSpec((MEM_LEN, MEM_WIDTH), lambda i: (i // per_batch, 1)),
                pl.BlockSpec((mix_w, D_MODEL), const), pl.BlockSpec((MEM_WIDTH, D_MODEL), const),
                pl.BlockSpec((1, D_MODEL), const), pl.BlockSpec((1, D_MODEL), const)]
    in_specs += [pl.BlockSpec((tm, mix_w), row) for _ in mixes]
    return pl.pallas_call(
        functools.partial(_post_kernel, n_groups=n_groups),
        grid=(t // tm,),
        in_specs=in_specs,
        out_specs=pl.BlockSpec((tm, D_MODEL), row),
        out_shape=jax.ShapeDtypeStruct((t, D_MODEL), F32),
        compiler_params=_params(1),
        name="post_mixer",
    )(x, qm, mkv, mkv, w_mix, w_mem, g, b, *mixes)


def _top_rows(s, n_rows, val_ref, idx_ref):
    iota = lax.broadcasted_iota(jnp.int32, s.shape, 0).astype(F32)
    for r in range(PEER_TOPK):
        m = jnp.max(s, axis=0, keepdims=True)
        am = jnp.min(jnp.where(s == m, iota, float(n_rows)), axis=0, keepdims=True)
        val_ref[r:r + 1, :] = m
        idx_ref[r:r + 1, :] = am
        s = jnp.where(iota == am, -jnp.inf, s)


def _route_kernel(x_ref, wq_ref, keys_ref, idx_ref, gate_ref, q_ref, tv_ref, ti_ref, bv_ref, bi_ref,
                  e_ref, gt_ref):
    tm = x_ref.shape[0]
    q_ref[...] = _dot(x_ref[...].astype(BF16), wq_ref[...]).astype(BF16)
    for h in range(PEER_HEADS):
        for p in range(2):
            c0 = (h * 2 + p) * PEER_HALF
            s = _dot_nt(keys_ref[p], q_ref[:, c0:c0 + PEER_HALF])
            _top_rows(s, PEER_N_KEYS, tv_ref.at[p], ti_ref.at[p])
        v1 = tv_ref[1]
        cand = jnp.concatenate([tv_ref[0, a:a + 1, :] + v1 for a in range(PEER_TOPK)], axis=0)
        _top_rows(cand, PEER_TOPK * PEER_TOPK, bv_ref, bi_ref)
        best_s = bv_ref[...]
        best_j = bi_ref[...].astype(jnp.int32)
        ja = best_j >> PEER_TOPK_SHIFT
        jb = best_j & (PEER_TOPK - 1)
        ia = jnp.zeros(best_j.shape, F32)
        ib = jnp.zeros(best_j.shape, F32)
        for a in range(PEER_TOPK):
            ia = jnp.where(ja == a, ti_ref[0, a:a + 1, :], ia)
            ib = jnp.where(jb == a, ti_ref[1, a:a + 1, :], ib)
        pe = jnp.exp(best_s - best_s[0:1, :])
        rs = slice(h * PEER_TOPK, (h + 1) * PEER_TOPK)
        e_ref[rs, :] = (ia.astype(jnp.int32) * PEER_N_KEYS + ib.astype(jnp.int32)) * PEER_ROW_WORDS
        gt_ref[rs, :] = pe / jnp.sum(pe, axis=0, keepdims=True)
    idx_ref[...] = pltpu.bitcast(pltpu.bitcast(e_ref[...], F32).T, jnp.int32)
    gate_ref[...] = gt_ref[...].T


def _peer_route(x, wq, keys, *, tm=256):
    t = x.shape[0]
    nq = wq.shape[1]
    return pl.pallas_call(
        _route_kernel,
        grid=(t // tm,),
        in_specs=[pl.BlockSpec((tm, D_MODEL), lambda i: (i, 0)),
                  pl.BlockSpec((D_MODEL, nq), lambda i: (0, 0)),
                  pl.BlockSpec((2, PEER_N_KEYS, PEER_HALF), lambda i: (0, 0, 0))],
        out_specs=[pl.BlockSpec((tm, PEER_SEL), lambda i: (i, 0))] * 2,
        out_shape=[jax.ShapeDtypeStruct((t, PEER_SEL), jnp.int32),
                   jax.ShapeDtypeStruct((t, PEER_SEL), F32)],
        scratch_shapes=[pltpu.VMEM((tm, nq), BF16),
                        pltpu.VMEM((2, PEER_TOPK, tm), F32), pltpu.VMEM((2, PEER_TOPK, tm), F32),
                        pltpu.VMEM((PEER_TOPK, tm), F32), pltpu.VMEM((PEER_TOPK, tm), F32),
                        pltpu.VMEM((PEER_SEL, tm), jnp.int32), pltpu.VMEM((PEER_SEL, tm), F32)],
        compiler_params=_params(1),
        name="peer_route",
    )(x, wq, keys)


def _pack_table(w):
    e = w.shape[0]
    wb = w.astype(BF16).reshape(e, PEER_ROW_WORDS, 2, LANES).transpose(0, 1, 3, 2)
    return lax.bitcast_convert_type(wb, jnp.uint32).reshape(e * PEER_ROW_WORDS, LANES)


def _gather_rows(idx_ref, t, tbl_ref, w_ref):
    for j in range(PEER_SEL):
        i = pl.multiple_of(idx_ref[t, j], PEER_ROW_WORDS)
        w_ref[PEER_ROW_WORDS * j:PEER_ROW_WORDS * (j + 1), :] = tbl_ref[pl.ds(i, PEER_ROW_WORDS), :]


def _staging_buffers():
    return [pltpu.VMEM((PEER_SEL * PEER_ROW_WORDS, LANES), jnp.uint32) for _ in range(PEER_STAGES)]


def _token_pipeline(idx_ref, tbl_ref, w_refs, compute, tb, unroll):
    n = PEER_STAGES
    assert len(w_refs) == n and unroll % n == 0 and tb % unroll == 0
    for s in range(n - 1):
        _gather_rows(idx_ref, s, tbl_ref, w_refs[s])

    def body(i, carry):
        t0 = unroll * i
        for s in range(unroll):
            compute(t0 + s, w_refs[s % n])
            ahead = jnp.minimum(t0 + s + n - 1, tb - 1)
            _gather_rows(idx_ref, ahead, tbl_ref, w_refs[(s + n - 1) % n])
        return carry

    lax.fori_loop(0, tb // unroll, body, 0)


def _peer_u_kernel(idx_ref, x8_ref, gate_ref, tbl_ref, a_ref, hb_ref, *w_refs, tb):
    width = PEER_SEL * PEER_CHUNKS
    lane = lax.broadcasted_iota(jnp.int32, (2 * PEER_CHUNKS, width), 1)
    row = lax.broadcasted_iota(jnp.int32, (2 * PEER_CHUNKS, width), 0)
    diag = (lane & (PEER_CHUNKS - 1)) == (row & (PEER_CHUNKS - 1))

    def compute(t, w_ref):
        wb = pltpu.bitcast(w_ref[...], BF16)
        x8 = x8_ref[pl.ds(pl.multiple_of(t * PEER_CHUNKS, PEER_CHUNKS), PEER_CHUNKS), :]
        xh = x8.astype(BF16)
        xl = (x8 - xh.astype(F32)).astype(BF16)
        r = _dot_nt(jnp.concatenate([xh, xl], axis=0), wb)
        hb_ref[pl.ds(t, 1), :] = jnp.sum(jnp.where(diag, r, 0.0), axis=0, keepdims=True)

    _token_pipeline(idx_ref, tbl_ref, w_refs, compute, tb, PEER_HIDDEN_BODY_TOKENS)
    gi = lax.broadcasted_iota(jnp.int32, (width, PEER_SEL), 0)
    gj = lax.broadcasted_iota(jnp.int32, (width, PEER_SEL), 1)
    group_sum = ((gi >> PEER_CHUNK_SHIFT) == gj).astype(BF16)
    h = jnp.zeros((tb, PEER_SEL), F32)
    for part in _split3(hb_ref[...]):
        h = h + _dot(part, group_sum)
    a_ref[...] = 0.5 * h * (1.0 + lax.erf(h * (2.0 ** -0.5))) * gate_ref[...]


def _peer_hidden(idx, x8, gate, tbl, *, tb=64):
    t = idx.shape[0]
    return pl.pallas_call(
        functools.partial(_peer_u_kernel, tb=tb),
        grid=(t // tb,),
        in_specs=[pl.BlockSpec((tb, PEER_SEL), lambda i: (i, 0), memory_space=pltpu.SMEM),
                  pl.BlockSpec((tb * PEER_CHUNKS, LANES), lambda i: (i, 0)),
                  pl.BlockSpec((tb, PEER_SEL), lambda i: (i, 0)),
                  pl.BlockSpec(memory_space=pltpu.VMEM)],
        out_specs=pl.BlockSpec((tb, PEER_SEL), lambda i: (i, 0)),
        out_shape=jax.ShapeDtypeStruct((t, PEER_SEL), F32),
        scratch_shapes=[pltpu.VMEM((tb, PEER_SEL * PEER_CHUNKS), F32)] + _staging_buffers(),
        compiler_params=_params(1),
        name="peer_hidden",
    )(idx, x8, gate, tbl)


def _peer_v_kernel(idx_ref, a_ref, tbl_ref, f8_ref, ah_ref, al_ref, *w_refs, tb):
    width = PEER_SEL * PEER_CHUNKS
    ei = lax.broadcasted_iota(jnp.int32, (PEER_SEL, width), 0)
    ej = lax.broadcasted_iota(jnp.int32, (PEER_SEL, width), 1)
    expand = (ei == (ej >> PEER_CHUNK_SHIFT)).astype(BF16)
    a = a_ref[...]
    ah = a.astype(BF16)
    al = (a - ah.astype(F32)).astype(BF16)
    ah_ref[...] = _dot(ah, expand)
    al_ref[...] = _dot(al, expand)
    lane = lax.broadcasted_iota(jnp.int32, (PEER_CHUNKS, width), 1)
    row = lax.broadcasted_iota(jnp.int32, (PEER_CHUNKS, width), 0)
    diag = (lane & (PEER_CHUNKS - 1)) == row

    def compute(t, w_ref):
        wb = pltpu.bitcast(w_ref[...], BF16)
        a8h = jnp.where(diag, ah_ref[pl.ds(t, 1), :], 0.0).astype(BF16)
        a8l = jnp.where(diag, al_ref[pl.ds(t, 1), :], 0.0).astype(BF16)
        r = _dot(jnp.concatenate([a8h, a8l], axis=0), wb)
        f8_ref[pl.ds(pl.multiple_of(t * PEER_CHUNKS, PEER_CHUNKS), PEER_CHUNKS), :] = (
            r[0:PEER_CHUNKS] + r[PEER_CHUNKS:2 * PEER_CHUNKS])

    _token_pipeline(idx_ref, tbl_ref, w_refs, compute, tb, PEER_OUTPUT_BODY_TOKENS)


def _peer_output(idx, a, tbl, *, tb=64):
    t = idx.shape[0]
    return pl.pallas_call(
        functools.partial(_peer_v_kernel, tb=tb),
        grid=(t // tb,),
        in_specs=[pl.BlockSpec((tb, PEER_SEL), lambda i: (i, 0), memory_space=pltpu.SMEM),
                  pl.BlockSpec((tb, PEER_SEL), lambda i: (i, 0)),
                  pl.BlockSpec(memory_space=pltpu.VMEM)],
        out_specs=pl.BlockSpec((tb * PEER_CHUNKS, LANES), lambda i: (i, 0)),
        out_shape=jax.ShapeDtypeStruct((t * PEER_CHUNKS, LANES), F32),
        scratch_shapes=[pltpu.VMEM((tb, PEER_SEL * PEER_CHUNKS), F32),
                        pltpu.VMEM((tb, PEER_SEL * PEER_CHUNKS), F32)] + _staging_buffers(),
        compiler_params=_params(1),
        name="peer_output",
    )(idx, a, tbl)


def _add_norm_kernel(x_ref, f_ref, g_ref, b_ref, o_ref):
    o_ref[...] = _layer_norm(DN_ALPHA * x_ref[...] + f_ref[...], g_ref[...], b_ref[...])


def _add_norm(x, f, g, b, *, tm=512):
    t = x.shape[0]
    row = lambda i: (i, 0)
    const = lambda i: (0, 0)
    return pl.pallas_call(
        _add_norm_kernel,
        grid=(t // tm,),
        in_specs=[pl.BlockSpec((tm, D_MODEL), row), pl.BlockSpec((tm, D_MODEL), row),
                  pl.BlockSpec((1, D_MODEL), const), pl.BlockSpec((1, D_MODEL), const)],
        out_specs=pl.BlockSpec((tm, D_MODEL), row),
        out_shape=jax.ShapeDtypeStruct((t, D_MODEL), F32),
        compiler_params=_params(1),
        name="add_norm",
    )(x, f, g, b)


def _peer(x, w_q, sub_keys, u, v, g, b):
    t = x.shape[0]
    idx, gate = _peer_route(x, w_q.astype(BF16), sub_keys.astype(BF16))
    a = _peer_hidden(idx, x.reshape(t * PEER_CHUNKS, LANES), gate, _pack_table(u))
    f = _peer_output(idx, a, _pack_table(v)).reshape(t, D_MODEL)
    return _add_norm(x, f, g, b)


def _pad_heads(w, heads, dim, dim_pad, axis=-1):
    axis = axis % w.ndim
    shape = w.shape[:axis] + (heads, dim) + w.shape[axis + 1:]
    pad = [(0, 0)] * (w.ndim + 1)
    pad[axis + 1] = (0, dim_pad - dim)
    out = jnp.pad(w.reshape(shape), pad)
    return out.reshape(w.shape[:axis] + (heads * dim_pad,) + w.shape[axis + 1:])


def kernel(x, mem, a_w_in, a_w_gate2, a_b_gate, a_norm_g, b_w_in, shared_w_kv, w_mem_kv, w_out, ln_mix_g,
           ln_mix_b, ln_ffn_g, ln_ffn_b, peer_w_q, peer_sub_keys, peer_u, peer_v):
    batch, seq, d = x.shape
    t = batch * seq
    xt = x.reshape(t, d)
    mem2 = mem.reshape(batch * MEM_LEN, d)
    row = lambda a: a.reshape(1, -1)
    qk_w = GLA_HEADS * GLA_DK
    v_w = GLA_HEADS * GLA_DV
    kd = None
    vd = None
    for l in range(DEPTH):
        (mkv,) = _matmul(mem2, w_mem_kv[l].astype(BF16), [2 * MEM_WIDTH], [F32])
        w_mem = w_out[l, v_w:].astype(BF16)
        if l < N_A_LAYERS:
            w = a_w_in[l]
            q0, k0, v0, r0 = 0, qk_w, 2 * qk_w, 2 * qk_w + v_w
            g0 = r0 + v_w
            m0 = g0 + GLA_GATE_RANK
            w_in = jnp.concatenate([
                _pad_heads(w[:, q0:k0], GLA_HEADS, GLA_DK, GLA_DK_PAD),
                _pad_heads(w[:, k0:v0], GLA_HEADS, GLA_DK, GLA_DK_PAD),
                _pad_heads(w[:, v0:r0], GLA_HEADS, GLA_DV, GLA_DV_PAD),
                _pad_heads(w[:, r0:g0], GLA_HEADS, GLA_DV, GLA_DV_PAD),
                w[:, m0:m0 + MEM_WIDTH],
                jnp.pad(w[:, g0:m0], ((0, 0), (0, LANES - GLA_GATE_RANK)))], axis=1).astype(BF16)
            qkw, vw = GLA_HEADS * GLA_DK_PAD, GLA_HEADS * GLA_DV_PAD
            q, k, v, r, qm, gin = _matmul(xt, w_in, [qkw, qkw, vw, vw, MEM_WIDTH, LANES], [F32] * 6)
            wg = jnp.pad(_pad_heads(a_w_gate2[l], GLA_HEADS, GLA_DK, GLA_DK_PAD),
                         ((0, LANES - GLA_GATE_RANK), (0, 0))).astype(BF16)
            bg = row(_pad_heads(a_b_gate[l], GLA_HEADS, GLA_DK, GLA_DK_PAD))
            ng = row(_pad_heads(a_norm_g[l], GLA_HEADS, GLA_DV, GLA_DV_PAD))
            mix = _gla(q, k, v, r, gin, wg, bg, ng, batch=batch, seq=seq)
            w_mix = _pad_heads(w_out[l, :v_w], GLA_HEADS, GLA_DV, GLA_DV_PAD, axis=0).astype(BF16)
            mixes, n_groups = [mix], 0
        else:
            w_in = b_w_in[l - N_A_LAYERS].astype(BF16)
            n_g = len(DIL_PAIRS)
            outs = _matmul(xt, w_in, [DIL_KV_WIDTH] * n_g + [MEM_WIDTH], [F32] * (n_g + 1))
            qm = outs[n_g]
            res = [_dilated_group(outs[g], kd, vd, batch=batch, seq=seq, window=wd, dilation=dl)
                   for g, (wd, dl) in enumerate(DIL_PAIRS)]
            mixes = [o for o, _ in res] + [s for _, s in res]
            n_groups = n_g
            w_mix = w_out[l, :DIL_KV_WIDTH].astype(BF16)
        xt = _post_mixer(xt, qm, mkv, w_mix, w_mem, row(ln_mix_g[l]), row(ln_mix_b[l]), mixes,
                         seq=seq, n_groups=n_groups)
        xt = _peer(xt, peer_w_q[l], peer_sub_keys[l], peer_u[l], peer_v[l], row(ln_ffn_g[l]), row(ln_ffn_b[l]))
        if l == N_A_LAYERS - 1:
            kd, vd = _matmul(xt, shared_w_kv.astype(BF16), [DIL_KV_WIDTH] * 2, [F32] * 2)
    return xt.reshape(batch, seq, d)
```

```python
import functools

import jax
import jax.numpy as jnp
from jax import lax
from jax.experimental import pallas as pl
from jax.experimental.pallas import tpu as pltpu

D_MODEL = 1024
DEPTH = 2
N_A_LAYERS = 1
DN_ALPHA = (2.0 * DEPTH) ** 0.25
LN_EPS = 1e-5
HEAD_NORM_EPS = 1e-6

MEM_LEN = 256
MEM_HEADS = 4
MEM_HEAD_DIM = 64
MEM_HEAD_SHIFT = 6
MEM_WIDTH = 256

GLA_HEADS = 4
GLA_DK = 96
GLA_DV = 192
GLA_DK_PAD = 128
GLA_DV_PAD = 256
GLA_GATE_RANK = 16
GLA_TAU = 16.0
GLA_CHUNK = 64
GLA_CHUNK_SHIFT = 6

DIL_PAIRS = ((128, 1), (512, 4), (2048, 16))
DIL_SLOTS = 6
DIL_HEAD_DIM = 128
DIL_BLOCK = 128
DIL_KV_WIDTH = DIL_SLOTS * DIL_HEAD_DIM

PEER_N_KEYS = 128
PEER_HEADS = 8
PEER_TOPK = 16
PEER_HALF = 128
PEER_SEL = PEER_HEADS * PEER_TOPK
PEER_ROW_WORDS = 4
PEER_CHUNKS = D_MODEL // 128
PEER_CHUNK_SHIFT = 3
PEER_STAGES = 4
PEER_HIDDEN_BODY_TOKENS = 16
PEER_OUTPUT_BODY_TOKENS = 16
PEER_TOPK_SHIFT = 4

LANES = 128
VMEM_LIMIT_BYTES = 56 * 1024 * 1024

BF16 = jnp.bfloat16
F32 = jnp.float32


def _params(n_grid_dims):
    return pltpu.CompilerParams(dimension_semantics=("arbitrary",) * n_grid_dims,
                                vmem_limit_bytes=VMEM_LIMIT_BYTES)


def _dot(a, b):
    return jnp.dot(a, b, preferred_element_type=F32)


def _dot_nt(a, b):
    return lax.dot_general(a, b, (((1,), (1,)), ((), ())), preferred_element_type=F32)


def _dot_tn(a, b):
    return lax.dot_general(a, b, (((0,), (0,)), ((), ())), preferred_element_type=F32)


def _split3(v):
    hi = v.astype(BF16)
    r1 = v - hi.astype(F32)
    mid = r1.astype(BF16)
    lo = (r1 - mid.astype(F32)).astype(BF16)
    return hi, mid, lo


def _layer_norm(z, g, b):
    mu = jnp.mean(z, axis=-1, keepdims=True)
    zc = z - mu
    var = jnp.mean(zc * zc, axis=-1, keepdims=True)
    return zc * lax.rsqrt(var + LN_EPS) * g + b


def _mm_kernel(x_ref, w_ref, *o_refs, splits):
    y = _dot(x_ref[...].astype(BF16), w_ref[...])
    off = 0
    for o_ref, width in zip(o_refs, splits):
        o_ref[...] = y[:, off:off + width].astype(o_ref.dtype)
        off += width


def _matmul(x, w, splits, dtypes, *, tm=256):
    t, k = x.shape
    n = w.shape[1]
    assert sum(splits) == n and t % tm == 0
    return pl.pallas_call(
        functools.partial(_mm_kernel, splits=tuple(splits)),
        grid=(t // tm,),
        in_specs=[pl.BlockSpec((tm, k), lambda i: (i, 0)),
                  pl.BlockSpec((k, n), lambda i: (0, 0))],
        out_specs=[pl.BlockSpec((tm, s), lambda i: (i, 0)) for s in splits],
        out_shape=[jax.ShapeDtypeStruct((t, s), d) for s, d in zip(splits, dtypes)],
        compiler_params=_params(1),
        name="matmul",
    )(x, w)


def _gla_kernel(q_ref, k_ref, v_ref, r_ref, g_ref, wg_ref, bg_ref, ng_ref, o_ref, state_ref, *, rows):
    nchunk = rows // GLA_CHUNK

    @pl.when(pl.program_id(1) == 0)
    def _():
        state_ref[...] = jnp.zeros_like(state_ref)

    g_pre = _dot(g_ref[...].astype(BF16), wg_ref[...]) + bg_ref[...]
    log_a = (jnp.minimum(g_pre, 0.0) - jnp.log1p(jnp.exp(-jnp.abs(g_pre)))) / GLA_TAU

    ri = lax.broadcasted_iota(jnp.int32, (rows, rows), 0)
    ci = lax.broadcasted_iota(jnp.int32, (rows, rows), 1)
    same_chunk_causal = (ci <= ri) & ((ri >> GLA_CHUNK_SHIFT) == (ci >> GLA_CHUNK_SHIFT))
    tril = same_chunk_causal.astype(BF16)
    b = jnp.zeros_like(log_a)
    for part in _split3(log_a):
        b = b + _dot(tril, part)
    b_last = jnp.concatenate(
        [jnp.broadcast_to(b[(c + 1) * GLA_CHUNK - 1:(c + 1) * GLA_CHUNK, :], (GLA_CHUNK, b.shape[1]))
         for c in range(nchunk)], axis=0)

    q = q_ref[...] * (GLA_DK ** -0.5)
    k = k_ref[...]
    q_t = (q * jnp.exp(b)).astype(BF16)
    k_t = (k * jnp.exp(-b)).astype(BF16)
    k_end = (k * jnp.exp(b_last - b)).astype(BF16)
    decay = jnp.exp(b_last)
    v = v_ref[...].astype(BF16)
    r = r_ref[...]
    ng = ng_ref[...]

    for h in range(GLA_HEADS):
        ks = slice(h * GLA_DK_PAD, (h + 1) * GLA_DK_PAD)
        vs = slice(h * GLA_DV_PAD, (h + 1) * GLA_DV_PAD)
        attn = jnp.where(same_chunk_causal, _dot_nt(q_t[:, ks], k_t[:, ks]), 0.0).astype(BF16)
        o_intra = _dot(attn, v[:, vs])
        st = state_ref[h]
        o_inter = []
        for c in range(nchunk):
            rs = slice(c * GLA_CHUNK, (c + 1) * GLA_CHUNK)
            o_inter.append(_dot_nt(q_t[rs, ks], st.astype(BF16)))
            st = st * decay[c * GLA_CHUNK:c * GLA_CHUNK + 1, ks] + _dot_tn(v[rs, vs], k_end[rs, ks])
        state_ref[h] = st
        o = o_intra + jnp.concatenate(o_inter, axis=0)
        ms = jnp.sum(o * o, axis=-1, keepdims=True) / GLA_DV
        o = o * lax.rsqrt(ms + HEAD_NORM_EPS) * ng[:, vs]
        rh = r[:, vs]
        o_ref[:, vs] = (rh * jax.nn.sigmoid(rh) * o).astype(o_ref.dtype)


def _gla(q, k, v, r, g, wg, bg, ng, *, batch, seq, rows=256):
    nblk = seq // rows
    row_map = lambda b, i: (b * nblk + i, 0)
    const = lambda b, i: (0, 0)
    qk_w, v_w = GLA_HEADS * GLA_DK_PAD, GLA_HEADS * GLA_DV_PAD
    return pl.pallas_call(
        functools.partial(_gla_kernel, rows=rows),
        grid=(batch, nblk),
        in_specs=[pl.BlockSpec((rows, qk_w), row_map), pl.BlockSpec((rows, qk_w), row_map),
                  pl.BlockSpec((rows, v_w), row_map), pl.BlockSpec((rows, v_w), row_map),
                  pl.BlockSpec((rows, LANES), row_map),
                  pl.BlockSpec((LANES, qk_w), const), pl.BlockSpec((1, qk_w), const),
                  pl.BlockSpec((1, v_w), const)],
        out_specs=pl.BlockSpec((rows, v_w), row_map),
        out_shape=jax.ShapeDtypeStruct((batch * seq, v_w), BF16),
        scratch_shapes=[pltpu.VMEM((GLA_HEADS, GLA_DV_PAD, GLA_DK_PAD), F32)],
        compiler_params=_params(2),
        name="gla",
    )(q, k, v, r, g, wg, bg, ng)


def _dilated_kernel(q_ref, kp_ref, kc_ref, vp_ref, vc_ref, o_ref, lse_ref, *, span, dilation, heads):
    first_key = jnp.where(pl.program_id(1) == 0, DIL_BLOCK, 0)
    qi = lax.broadcasted_iota(jnp.int32, (DIL_BLOCK, 2 * DIL_BLOCK), 0)
    kj = lax.broadcasted_iota(jnp.int32, (DIL_BLOCK, 2 * DIL_BLOCK), 1)
    rel = DIL_BLOCK + qi - kj
    mask = (rel >= 0) & (rel <= span) & (kj >= first_key)
    scale = DIL_HEAD_DIM ** -0.5
    for r in range(dilation):
        rows = pl.ds(r, DIL_BLOCK, stride=dilation) if dilation > 1 else slice(None)
        for h in range(heads):
            hs = slice(h * DIL_HEAD_DIM, (h + 1) * DIL_HEAD_DIM)
            q = q_ref[rows, hs].astype(BF16)
            kcat = jnp.concatenate([kp_ref[rows, hs], kc_ref[rows, hs]], axis=0).astype(BF16)
            vcat = jnp.concatenate([vp_ref[rows, hs], vc_ref[rows, hs]], axis=0).astype(BF16)
            s = jnp.where(mask, _dot_nt(q, kcat) * scale, -jnp.inf)
            m = jnp.max(s, axis=-1, keepdims=True)
            p = jnp.exp(s - m)
            l = jnp.sum(p, axis=-1, keepdims=True)
            o_ref[rows, hs] = _dot(p.astype(BF16), vcat) / l
            lse_ref[rows, hs] = jnp.broadcast_to(m + jnp.log(l), (DIL_BLOCK, DIL_HEAD_DIM))


def _dilated_heads_per_step(dilation):
    return DIL_SLOTS if dilation == 1 else 1


def _dilated_group(q, k, v, *, batch, seq, window, dilation, heads_per_step):
    rows = dilation * DIL_BLOCK
    nb = seq // rows
    cur = lambda b, n, hg: (b * nb + n, hg)
    prev = lambda b, n, hg: (b * nb + jnp.maximum(n - 1, 0), hg)
    blk = (rows, heads_per_step * DIL_HEAD_DIM)
    return pl.pallas_call(
        functools.partial(_dilated_kernel, span=window // dilation, dilation=dilation, heads=heads_per_step),
        grid=(batch, nb, DIL_SLOTS // heads_per_step),
        in_specs=[pl.BlockSpec(blk, cur), pl.BlockSpec(blk, prev), pl.BlockSpec(blk, cur),
                  pl.BlockSpec(blk, prev), pl.BlockSpec(blk, cur)],
        out_specs=[pl.BlockSpec(blk, cur), pl.BlockSpec(blk, cur)],
        out_shape=[jax.ShapeDtypeStruct((batch * seq, DIL_KV_WIDTH), F32)] * 2,
        compiler_params=_params(3),
        name="dilated_attention",
    )(q, k, k, v, v)


def _post_kernel(*refs, n_groups):
    x_ref, qm_ref, km_ref, vm_ref, wmix_ref, wmem_ref, g_ref, b_ref = refs[:8]
    mix_refs = refs[8:-1]
    o_ref = refs[-1]
    if n_groups == 0:
        mix = mix_refs[0][...]
    else:
        lses = [mix_refs[n_groups + g][...] for g in range(n_groups)]
        mx = functools.reduce(jnp.maximum, lses)
        ws = [jnp.exp(l - mx) for l in lses]
        den = functools.reduce(lambda a, c: a + c, ws)
        mix = functools.reduce(lambda a, c: a + c, [w * mix_refs[g][...] for g, w in enumerate(ws)]) / den
    qm = qm_ref[...]
    km = km_ref[...].astype(BF16)
    vm = vm_ref[...].astype(BF16)
    lane = lax.broadcasted_iota(jnp.int32, qm.shape, 1)
    mo = jnp.zeros(qm.shape, F32)
    for h in range(MEM_HEADS):
        head = (lane >> MEM_HEAD_SHIFT) == h
        s = _dot_nt(jnp.where(head, qm, 0.0).astype(BF16), km) * (MEM_HEAD_DIM ** -0.5)
        m = jnp.max(s, axis=-1, keepdims=True)
        p = jnp.exp(s - m)
        p = p / jnp.sum(p, axis=-1, keepdims=True)
        mo = mo + jnp.where(head, _dot(p.astype(BF16), vm), 0.0)
    y = _dot(mix.astype(BF16), wmix_ref[...]) + _dot(mo.astype(BF16), wmem_ref[...])
    o_ref[...] = _layer_norm(DN_ALPHA * x_ref[...] + y, g_ref[...], b_ref[...])


def _post_mixer(x, qm, mkv, w_mix, w_mem, g, b, mixes, *, seq, n_groups, tm=256):
    t = x.shape[0]
    per_batch = seq // tm
    row = lambda i: (i, 0)
    const = lambda i: (0, 0)
    mix_w = w_mix.shape[0]
    in_specs = [pl.BlockSpec((tm, D_MODEL), row), pl.BlockSpec((tm, MEM_WIDTH), row),
                pl.BlockSpec((MEM_LEN, MEM_WIDTH), lambda i: (i // per_batch, 0)),
                pl.BlockSpec((MEM_LEN, MEM_WIDTH), lambda i: (i // per_batch, 1)),
                pl.BlockSpec((mix_w, D_MODEL), const), pl.BlockSpec((MEM_WIDTH, D_MODEL), const),
                pl.BlockSpec((1, D_MODEL), const), pl.BlockSpec((1, D_MODEL), const)]
    in_specs += [pl.BlockSpec((tm, mix_w), row) for _ in mixes]
    return pl.pallas_call(
        functools.partial(_post_kernel, n_groups=n_groups),
        grid=(t // tm,),
        in_specs=in_specs,
        out_specs=pl.BlockSpec((tm, D_MODEL), row),
        out_shape=jax.ShapeDtypeStruct((t, D_MODEL), F32),
        compiler_params=_params(1),
        name="post_mixer",
    )(x, qm, mkv, mkv, w_mix, w_mem, g, b, *mixes)


def _top_rows(s, n_rows, val_ref, idx_ref):
    iota = lax.broadcasted_iota(jnp.int32, s.shape, 0).astype(F32)
    for r in range(PEER_TOPK):
        m = jnp.max(s, axis=0, keepdims=True)
        am = jnp.min(jnp.where(s == m, iota, float(n_rows)), axis=0, keepdims=True)
        val_ref[r:r + 1, :] = m
        idx_ref[r:r + 1, :] = am
        s = jnp.where(iota == am, -jnp.inf, s)


def _route_kernel(x_ref, wq_ref, keys_ref, idx_ref, gate_ref, q_ref, tv_ref, ti_ref, bv_ref, bi_ref,
                  e_ref, gt_ref):
    tm = x_ref.shape[0]
    q_ref[...] = _dot(x_ref[...].astype(BF16), wq_ref[...]).astype(BF16)
    for h in range(PEER_HEADS):
        for p in range(2):
            c0 = (h * 2 + p) * PEER_HALF
            s = _dot_nt(keys_ref[p], q_ref[:, c0:c0 + PEER_HALF])
            _top_rows(s, PEER_N_KEYS, tv_ref.at[p], ti_ref.at[p])
        v1 = tv_ref[1]
        cand = jnp.concatenate([tv_ref[0, a:a + 1, :] + v1 for a in range(PEER_TOPK)], axis=0)
        _top_rows(cand, PEER_TOPK * PEER_TOPK, bv_ref, bi_ref)
        best_s = bv_ref[...]
        best_j = bi_ref[...].astype(jnp.int32)
        ja = best_j >> PEER_TOPK_SHIFT
        jb = best_j & (PEER_TOPK - 1)
        ia = jnp.zeros(best_j.shape, F32)
        ib = jnp.zeros(best_j.shape, F32)
        for a in range(PEER_TOPK):
            ia = jnp.where(ja == a, ti_ref[0, a:a + 1, :], ia)
            ib = jnp.where(jb == a, ti_ref[1, a:a + 1, :], ib)
        pe = jnp.exp(best_s - best_s[0:1, :])
        rs = slice(h * PEER_TOPK, (h + 1) * PEER_TOPK)
        e_ref[rs, :] = (ia.astype(jnp.int32) * PEER_N_KEYS + ib.astype(jnp.int32)) * PEER_ROW_WORDS
        gt_ref[rs, :] = pe / jnp.sum(pe, axis=0, keepdims=True)
    idx_ref[...] = pltpu.bitcast(pltpu.bitcast(e_ref[...], F32).T, jnp.int32)
    gate_ref[...] = gt_ref[...].T


def _peer_route(x, wq, keys, *, tm=256):
    t = x.shape[0]
    nq = wq.shape[1]
    return pl.pallas_call(
        _route_kernel,
        grid=(t // tm,),
        in_specs=[pl.BlockSpec((tm, D_MODEL), lambda i: (i, 0)),
                  pl.BlockSpec((D_MODEL, nq), lambda i: (0, 0)),
                  pl.BlockSpec((2, PEER_N_KEYS, PEER_HALF), lambda i: (0, 0, 0))],
        out_specs=[pl.BlockSpec((tm, PEER_SEL), lambda i: (i, 0))] * 2,
        out_shape=[jax.ShapeDtypeStruct((t, PEER_SEL), jnp.int32),
                   jax.ShapeDtypeStruct((t, PEER_SEL), F32)],
        scratch_shapes=[pltpu.VMEM((tm, nq), BF16),
                        pltpu.VMEM((2, PEER_TOPK, tm), F32), pltpu.VMEM((2, PEER_TOPK, tm), F32),
                        pltpu.VMEM((PEER_TOPK, tm), F32), pltpu.VMEM((PEER_TOPK, tm), F32),
                        pltpu.VMEM((PEER_SEL, tm), jnp.int32), pltpu.VMEM((PEER_SEL, tm), F32)],
        compiler_params=_params(1),
        name="peer_route",
    )(x, wq, keys)


def _pack_table(w):
    e = w.shape[0]
    wb = w.astype(BF16).reshape(e, PEER_ROW_WORDS, 2, LANES).transpose(0, 1, 3, 2)
    return lax.bitcast_convert_type(wb, jnp.uint32).reshape(e * PEER_ROW_WORDS, LANES)


def _gather_rows(idx_ref, t, tbl_ref, w_ref):
    for j in range(PEER_SEL):
        i = pl.multiple_of(idx_ref[t, j], PEER_ROW_WORDS)
        w_ref[PEER_ROW_WORDS * j:PEER_ROW_WORDS * (j + 1), :] = tbl_ref[pl.ds(i, PEER_ROW_WORDS), :]


def _staging_buffers():
    return [pltpu.VMEM((PEER_SEL * PEER_ROW_WORDS, LANES), jnp.uint32) for _ in range(PEER_STAGES)]


def _token_pipeline(idx_ref, tbl_ref, w_refs, compute, tb, unroll):
    n = PEER_STAGES
    assert len(w_refs) == n and unroll % n == 0 and tb % unroll == 0
    for s in range(n - 1):
        _gather_rows(idx_ref, s, tbl_ref, w_refs[s])

    def body(i, carry):
        t0 = unroll * i
        for s in range(unroll):
            compute(t0 + s, w_refs[s % n])
            ahead = jnp.minimum(t0 + s + n - 1, tb - 1)
            _gather_rows(idx_ref, ahead, tbl_ref, w_refs[(s + n - 1) % n])
        return carry

    lax.fori_loop(0, tb // unroll, body, 0)


def _peer_u_kernel(idx_ref, x_ref, gate_ref, tbl_ref, a_ref, hb_ref, x8_ref, *w_refs, tb):
    width = PEER_SEL * PEER_CHUNKS
    for c in range(PEER_CHUNKS):
        x8_ref[pl.ds(c, tb, stride=PEER_CHUNKS), :] = x_ref[:, c * LANES:(c + 1) * LANES]
    lane = lax.broadcasted_iota(jnp.int32, (2 * PEER_CHUNKS, width), 1)
    row = lax.broadcasted_iota(jnp.int32, (2 * PEER_CHUNKS, width), 0)
    diag = (lane & (PEER_CHUNKS - 1)) == (row & (PEER_CHUNKS - 1))

    def compute(t, w_ref):
        wb = pltpu.bitcast(w_ref[...], BF16)
        x8 = x8_ref[pl.ds(pl.multiple_of(t * PEER_CHUNKS, PEER_CHUNKS), PEER_CHUNKS), :]
        xh = x8.astype(BF16)
        xl = (x8 - xh.astype(F32)).astype(BF16)
        r = _dot_nt(jnp.concatenate([xh, xl], axis=0), wb)
        hb_ref[pl.ds(t, 1), :] = jnp.sum(jnp.where(diag, r, 0.0), axis=0, keepdims=True)

    _token_pipeline(idx_ref, tbl_ref, w_refs, compute, tb, PEER_HIDDEN_BODY_TOKENS)
    gi = lax.broadcasted_iota(jnp.int32, (width, PEER_SEL), 0)
    gj = lax.broadcasted_iota(jnp.int32, (width, PEER_SEL), 1)
    group_sum = ((gi >> PEER_CHUNK_SHIFT) == gj).astype(BF16)
    h = jnp.zeros((tb, PEER_SEL), F32)
    for part in _split3(hb_ref[...]):
        h = h + _dot(part, group_sum)
    a_ref[...] = 0.5 * h * (1.0 + lax.erf(h * (2.0 ** -0.5))) * gate_ref[...]


def _peer_hidden(idx, x, gate, tbl, *, tb=64):
    t = idx.shape[0]
    return pl.pallas_call(
        functools.partial(_peer_u_kernel, tb=tb),
        grid=(t // tb,),
        in_specs=[pl.BlockSpec((tb, PEER_SEL), lambda i: (i, 0), memory_space=pltpu.SMEM),
                  pl.BlockSpec((tb, D_MODEL), lambda i: (i, 0)),
                  pl.BlockSpec((tb, PEER_SEL), lambda i: (i, 0)),
                  pl.BlockSpec(memory_space=pltpu.VMEM)],
        out_specs=pl.BlockSpec((tb, PEER_SEL), lambda i: (i, 0)),
        out_shape=jax.ShapeDtypeStruct((t, PEER_SEL), F32),
        scratch_shapes=[pltpu.VMEM((tb, PEER_SEL * PEER_CHUNKS), F32),
                        pltpu.VMEM((tb * PEER_CHUNKS, LANES), F32)] + _staging_buffers(),
        compiler_params=_params(1),
        name="peer_hidden",
    )(idx, x, gate, tbl)


def _peer_v_kernel(idx_ref, a_ref, tbl_ref, f8_ref, ah_ref, al_ref, *w_refs, tb):
    width = PEER_SEL * PEER_CHUNKS
    ei = lax.broadcasted_iota(jnp.int32, (PEER_SEL, width), 0)
    ej = lax.broadcasted_iota(jnp.int32, (PEER_SEL, width), 1)
    expand = (ei == (ej >> PEER_CHUNK_SHIFT)).astype(BF16)
    a = a_ref[...]
    ah = a.astype(BF16)
    al = (a - ah.astype(F32)).astype(BF16)
    ah_ref[...] = _dot(ah, expand)
    al_ref[...] = _dot(al, expand)
    lane = lax.broadcasted_iota(jnp.int32, (PEER_CHUNKS, width), 1)
    row = lax.broadcasted_iota(jnp.int32, (PEER_CHUNKS, width), 0)
    diag = (lane & (PEER_CHUNKS - 1)) == row

    def compute(t, w_ref):
        wb = pltpu.bitcast(w_ref[...], BF16)
        a8h = jnp.where(diag, ah_ref[pl.ds(t, 1), :], 0.0).astype(BF16)
        a8l = jnp.where(diag, al_ref[pl.ds(t, 1), :], 0.0).astype(BF16)
        r = _dot(jnp.concatenate([a8h, a8l], axis=0), wb)
        f8_ref[pl.ds(pl.multiple_of(t * PEER_CHUNKS, PEER_CHUNKS), PEER_CHUNKS), :] = (
            r[0:PEER_CHUNKS] + r[PEER_CHUNKS:2 * PEER_CHUNKS])

    _token_pipeline(idx_ref, tbl_ref, w_refs, compute, tb, PEER_OUTPUT_BODY_TOKENS)


def _peer_output(idx, a, tbl, *, tb=64):
    t = idx.shape[0]
    return pl.pallas_call(
        functools.partial(_peer_v_kernel, tb=tb),
        grid=(t // tb,),
        in_specs=[pl.BlockSpec((tb, PEER_SEL), lambda i: (i, 0), memory_space=pltpu.SMEM),
                  pl.BlockSpec((tb, PEER_SEL), lambda i: (i, 0)),
                  pl.BlockSpec(memory_space=pltpu.VMEM)],
        out_specs=pl.BlockSpec((tb * PEER_CHUNKS, LANES), lambda i: (i, 0)),
        out_shape=jax.ShapeDtypeStruct((t * PEER_CHUNKS, LANES), F32),
        scratch_shapes=[pltpu.VMEM((tb, PEER_SEL * PEER_CHUNKS), F32),
                        pltpu.VMEM((tb, PEER_SEL * PEER_CHUNKS), F32)] + _staging_buffers(),
        compiler_params=_params(1),
        name="peer_output",
    )(idx, a, tbl)


def _add_norm_kernel(x_ref, f8_ref, g_ref, b_ref, o_ref):
    tm = x_ref.shape[0]
    f = jnp.concatenate([f8_ref[pl.ds(c, tm, stride=PEER_CHUNKS), :] for c in range(PEER_CHUNKS)], axis=1)
    o_ref[...] = _layer_norm(DN_ALPHA * x_ref[...] + f, g_ref[...], b_ref[...])


def _add_norm(x, f8, g, b, *, tm=512):
    t = x.shape[0]
    row = lambda i: (i, 0)
    const = lambda i: (0, 0)
    return pl.pallas_call(
        _add_norm_kernel,
        grid=(t // tm,),
        in_specs=[pl.BlockSpec((tm, D_MODEL), row), pl.BlockSpec((tm * PEER_CHUNKS, LANES), row),
                  pl.BlockSpec((1, D_MODEL), const), pl.BlockSpec((1, D_MODEL), const)],
        out_specs=pl.BlockSpec((tm, D_MODEL), row),
        out_shape=jax.ShapeDtypeStruct((t, D_MODEL), F32),
        compiler_params=_params(1),
        name="add_norm",
    )(x, f8, g, b)


def _peer(x, w_q, sub_keys, u, v, g, b):
    idx, gate = _peer_route(x, w_q.astype(BF16), sub_keys.astype(BF16))
    a = _peer_hidden(idx, x, gate, _pack_table(u))
    f8 = _peer_output(idx, a, _pack_table(v))
    return _add_norm(x, f8, g, b)


def _pad_heads(w, heads, dim, dim_pad, axis=-1):
    axis = axis % w.ndim
    shape = w.shape[:axis] + (heads, dim) + w.shape[axis + 1:]
    pad = [(0, 0)] * (w.ndim + 1)
    pad[axis + 1] = (0, dim_pad - dim)
    out = jnp.pad(w.reshape(shape), pad)
    return out.reshape(w.shape[:axis] + (heads * dim_pad,) + w.shape[axis + 1:])


def kernel(x, mem, a_w_in, a_w_gate2, a_b_gate, a_norm_g, b_w_in, shared_w_kv, w_mem_kv, w_out, ln_mix_g,
           ln_mix_b, ln_ffn_g, ln_ffn_b, peer_w_q, peer_sub_keys, peer_u, peer_v):
    batch, seq, d = x.shape
    t = batch * seq
    xt = x.reshape(t, d)
    mem2 = mem.reshape(batch * MEM_LEN, d)
    row = lambda a: a.reshape(1, -1)
    qk_w = GLA_HEADS * GLA_DK
    v_w = GLA_HEADS * GLA_DV
    kd = None
    vd = None
    for l in range(DEPTH):
        (mkv,) = _matmul(mem2, w_mem_kv[l].astype(BF16), [2 * MEM_WIDTH], [F32])
        w_mem = w_out[l, v_w:].astype(BF16)
        if l < N_A_LAYERS:
            w = a_w_in[l]
            q0, k0, v0, r0 = 0, qk_w, 2 * qk_w, 2 * qk_w + v_w
            g0 = r0 + v_w
            m0 = g0 + GLA_GATE_RANK
            w_in = jnp.concatenate([
                _pad_heads(w[:, q0:k0], GLA_HEADS, GLA_DK, GLA_DK_PAD),
                _pad_heads(w[:, k0:v0], GLA_HEADS, GLA_DK, GLA_DK_PAD),
                _pad_heads(w[:, v0:r0], GLA_HEADS, GLA_DV, GLA_DV_PAD),
                _pad_heads(w[:, r0:g0], GLA_HEADS, GLA_DV, GLA_DV_PAD),
                w[:, m0:m0 + MEM_WIDTH],
                jnp.pad(w[:, g0:m0], ((0, 0), (0, LANES - GLA_GATE_RANK)))], axis=1).astype(BF16)
            qkw, vw = GLA_HEADS * GLA_DK_PAD, GLA_HEADS * GLA_DV_PAD
            q, k, v, r, qm, gin = _matmul(xt, w_in, [qkw, qkw, vw, vw, MEM_WIDTH, LANES], [F32] * 6)
            wg = jnp.pad(_pad_heads(a_w_gate2[l], GLA_HEADS, GLA_DK, GLA_DK_PAD),
                         ((0, LANES - GLA_GATE_RANK), (0, 0))).astype(BF16)
            bg = row(_pad_heads(a_b_gate[l], GLA_HEADS, GLA_DK, GLA_DK_PAD))
            ng = row(_pad_heads(a_norm_g[l], GLA_HEADS, GLA_DV, GLA_DV_PAD))
            mix = _gla(q, k, v, r, gin, wg, bg, ng, batch=batch, seq=seq)
            w_mix = _pad_heads(w_out[l, :v_w], GLA_HEADS, GLA_DV, GLA_DV_PAD, axis=0).astype(BF16)
            mixes, n_groups = [mix], 0
        else:
            w_in = b_w_in[l - N_A_LAYERS].astype(BF16)
            n_g = len(DIL_PAIRS)
            outs = _matmul(xt, w_in, [DIL_KV_WIDTH] * n_g + [MEM_WIDTH], [F32] * (n_g + 1))
            qm = outs[n_g]
            res = [_dilated_group(outs[g], kd, vd, batch=batch, seq=seq, window=wd, dilation=dl,
                                  heads_per_step=_dilated_heads_per_step(dl))
                   for g, (wd, dl) in enumerate(DIL_PAIRS)]
            mixes = [o for o, _ in res] + [s for _, s in res]
            n_groups = n_g
            w_mix = w_out[l, :DIL_KV_WIDTH].astype(BF16)
        xt = _post_mixer(xt, qm, mkv, w_mix, w_mem, row(ln_mix_g[l]), row(ln_mix_b[l]), mixes,
                         seq=seq, n_groups=n_groups)
        xt = _peer(xt, peer_w_q[l], peer_sub_keys[l], peer_u[l], peer_v[l], row(ln_ffn_g[l]), row(ln_ffn_b[l]))
        if l == N_A_LAYERS - 1:
            kd, vd = _matmul(xt, shared_w_kv.astype(BF16), [DIL_KV_WIDTH] * 2, [F32] * 2)
    return xt.reshape(batch, seq, d)
```

```python
import functools

import jax
import jax.numpy as jnp
from jax import lax
from jax.experimental import pallas as pl
from jax.experimental.pallas import tpu as pltpu

D_MODEL = 1024
DEPTH = 2
N_A_LAYERS = 1
DN_ALPHA = (2.0 * DEPTH) ** 0.25
LN_EPS = 1e-5
HEAD_NORM_EPS = 1e-6

MEM_LEN = 256
MEM_HEADS = 4
MEM_HEAD_DIM = 64
MEM_HEAD_SHIFT = 6
MEM_WIDTH = 256

GLA_HEADS = 4
GLA_DK = 96
GLA_DV = 192
GLA_DK_PAD = 128
GLA_DV_PAD = 256
GLA_GATE_RANK = 16
GLA_TAU = 16.0
GLA_CHUNK = 64
GLA_CHUNK_SHIFT = 6

DIL_PAIRS = ((128, 1), (512, 4), (2048, 16))
DIL_SLOTS = 6
DIL_HEAD_DIM = 128
DIL_BLOCK = 128
DIL_KV_WIDTH = DIL_SLOTS * DIL_HEAD_DIM

PEER_N_KEYS = 128
PEER_HEADS = 8
PEER_TOPK = 16
PEER_HALF = 128
PEER_SEL = PEER_HEADS * PEER_TOPK
PEER_ROW_WORDS = 4
PEER_CHUNKS = D_MODEL // 128
PEER_CHUNK_SHIFT = 3
PEER_STAGES = 4
PEER_HIDDEN_BODY_TOKENS = 16
PEER_OUTPUT_BODY_TOKENS = 16
PEER_TOPK_SHIFT = 4

LANES = 128
VMEM_LIMIT_BYTES = 56 * 1024 * 1024

BF16 = jnp.bfloat16
F32 = jnp.float32


def _params(n_grid_dims):
    return pltpu.CompilerParams(dimension_semantics=("arbitrary",) * n_grid_dims,
                                vmem_limit_bytes=VMEM_LIMIT_BYTES)


def _dot(a, b):
    return jnp.dot(a, b, preferred_element_type=F32)


def _dot_nt(a, b):
    return lax.dot_general(a, b, (((1,), (1,)), ((), ())), preferred_element_type=F32)


def _dot_tn(a, b):
    return lax.dot_general(a, b, (((0,), (0,)), ((), ())), preferred_element_type=F32)


def _split3(v):
    hi = v.astype(BF16)
    r1 = v - hi.astype(F32)
    mid = r1.astype(BF16)
    lo = (r1 - mid.astype(F32)).astype(BF16)
    return hi, mid, lo


def _layer_norm(z, g, b):
    mu = jnp.mean(z, axis=-1, keepdims=True)
    zc = z - mu
    var = jnp.mean(zc * zc, axis=-1, keepdims=True)
    return zc * lax.rsqrt(var + LN_EPS) * g + b


def _mm_kernel(x_ref, w_ref, *o_refs, splits):
    y = _dot(x_ref[...].astype(BF16), w_ref[...])
    off = 0
    for o_ref, width in zip(o_refs, splits):
        o_ref[...] = y[:, off:off + width].astype(o_ref.dtype)
        off += width


def _matmul(x, w, splits, dtypes, *, tm=256):
    t, k = x.shape
    n = w.shape[1]
    assert sum(splits) == n and t % tm == 0
    return pl.pallas_call(
        functools.partial(_mm_kernel, splits=tuple(splits)),
        grid=(t // tm,),
        in_specs=[pl.BlockSpec((tm, k), lambda i: (i, 0)),
                  pl.BlockSpec((k, n), lambda i: (0, 0))],
        out_specs=[pl.BlockSpec((tm, s), lambda i: (i, 0)) for s in splits],
        out_shape=[jax.ShapeDtypeStruct((t, s), d) for s, d in zip(splits, dtypes)],
        compiler_params=_params(1),
        name="matmul",
    )(x, w)


def _gla_kernel(q_ref, k_ref, v_ref, r_ref, g_ref, wg_ref, bg_ref, ng_ref, o_ref, state_ref, *, rows):
    nchunk = rows // GLA_CHUNK

    @pl.when(pl.program_id(1) == 0)
    def _():
        state_ref[...] = jnp.zeros_like(state_ref)

    g_pre = _dot(g_ref[...].astype(BF16), wg_ref[...]) + bg_ref[...]
    log_a = (jnp.minimum(g_pre, 0.0) - jnp.log1p(jnp.exp(-jnp.abs(g_pre)))) / GLA_TAU

    ri = lax.broadcasted_iota(jnp.int32, (rows, rows), 0)
    ci = lax.broadcasted_iota(jnp.int32, (rows, rows), 1)
    same_chunk_causal = (ci <= ri) & ((ri >> GLA_CHUNK_SHIFT) == (ci >> GLA_CHUNK_SHIFT))
    tril = same_chunk_causal.astype(BF16)
    b = jnp.zeros_like(log_a)
    for part in _split3(log_a):
        b = b + _dot(tril, part)
    b_last = jnp.concatenate(
        [jnp.broadcast_to(b[(c + 1) * GLA_CHUNK - 1:(c + 1) * GLA_CHUNK, :], (GLA_CHUNK, b.shape[1]))
         for c in range(nchunk)], axis=0)

    q = q_ref[...] * (GLA_DK ** -0.5)
    k = k_ref[...]
    q_t = (q * jnp.exp(b)).astype(BF16)
    k_t = (k * jnp.exp(-b)).astype(BF16)
    k_end = (k * jnp.exp(b_last - b)).astype(BF16)
    decay = jnp.exp(b_last)
    v = v_ref[...].astype(BF16)
    r = r_ref[...]
    ng = ng_ref[...]

    for h in range(GLA_HEADS):
        ks = slice(h * GLA_DK_PAD, (h + 1) * GLA_DK_PAD)
        vs = slice(h * GLA_DV_PAD, (h + 1) * GLA_DV_PAD)
        attn = jnp.where(same_chunk_causal, _dot_nt(q_t[:, ks], k_t[:, ks]), 0.0).astype(BF16)
        o_intra = _dot(attn, v[:, vs])
        st = state_ref[h]
        o_inter = []
        for c in range(nchunk):
            rs = slice(c * GLA_CHUNK, (c + 1) * GLA_CHUNK)
            o_inter.append(_dot_nt(q_t[rs, ks], st.astype(BF16)))
            st = st * decay[c * GLA_CHUNK:c * GLA_CHUNK + 1, ks] + _dot_tn(v[rs, vs], k_end[rs, ks])
        state_ref[h] = st
        o = o_intra + jnp.concatenate(o_inter, axis=0)
        ms = jnp.sum(o * o, axis=-1, keepdims=True) / GLA_DV
        o = o * lax.rsqrt(ms + HEAD_NORM_EPS) * ng[:, vs]
        rh = r[:, vs]
        o_ref[:, vs] = (rh * jax.nn.sigmoid(rh) * o).astype(o_ref.dtype)


def _gla(q, k, v, r, g, wg, bg, ng, *, batch, seq, rows=256):
    nblk = seq // rows
    row_map = lambda b, i: (b * nblk + i, 0)
    const = lambda b, i: (0, 0)
    qk_w, v_w = GLA_HEADS * GLA_DK_PAD, GLA_HEADS * GLA_DV_PAD
    return pl.pallas_call(
        functools.partial(_gla_kernel, rows=rows),
        grid=(batch, nblk),
        in_specs=[pl.BlockSpec((rows, qk_w), row_map), pl.BlockSpec((rows, qk_w), row_map),
                  pl.BlockSpec((rows, v_w), row_map), pl.BlockSpec((rows, v_w), row_map),
                  pl.BlockSpec((rows, LANES), row_map),
                  pl.BlockSpec((LANES, qk_w), const), pl.BlockSpec((1, qk_w), const),
                  pl.BlockSpec((1, v_w), const)],
        out_specs=pl.BlockSpec((rows, v_w), row_map),
        out_shape=jax.ShapeDtypeStruct((batch * seq, v_w), BF16),
        scratch_shapes=[pltpu.VMEM((GLA_HEADS, GLA_DV_PAD, GLA_DK_PAD), F32)],
        compiler_params=_params(2),
        name="gla",
    )(q, k, v, r, g, wg, bg, ng)


def _dilated_kernel(q_ref, kp_ref, kc_ref, vp_ref, vc_ref, o_ref, lse_ref, *, span, dilation, heads):
    first_key = jnp.where(pl.program_id(1) == 0, DIL_BLOCK, 0)
    qi = lax.broadcasted_iota(jnp.int32, (DIL_BLOCK, 2 * DIL_BLOCK), 0)
    kj = lax.broadcasted_iota(jnp.int32, (DIL_BLOCK, 2 * DIL_BLOCK), 1)
    rel = DIL_BLOCK + qi - kj
    mask = (rel >= 0) & (rel <= span) & (kj >= first_key)
    scale = DIL_HEAD_DIM ** -0.5
    for r in range(dilation):
        rows = pl.ds(r, DIL_BLOCK, stride=dilation) if dilation > 1 else slice(None)
        for h in range(heads):
            hs = slice(h * DIL_HEAD_DIM, (h + 1) * DIL_HEAD_DIM)
            q = q_ref[rows, hs].astype(BF16)
            kcat = jnp.concatenate([kp_ref[rows, hs], kc_ref[rows, hs]], axis=0).astype(BF16)
            vcat = jnp.concatenate([vp_ref[rows, hs], vc_ref[rows, hs]], axis=0).astype(BF16)
            s = jnp.where(mask, _dot_nt(q, kcat) * scale, -jnp.inf)
            m = jnp.max(s, axis=-1, keepdims=True)
            p = jnp.exp(s - m)
            l = jnp.sum(p, axis=-1, keepdims=True)
            o_ref[rows, hs] = _dot(p.astype(BF16), vcat) / l
            lse_ref[rows, hs] = jnp.broadcast_to(m + jnp.log(l), (DIL_BLOCK, DIL_HEAD_DIM))


def _dilated_heads_per_step(dilation):
    return DIL_SLOTS if dilation == 1 else 1


def _dilated_group(q, k, v, *, batch, seq, window, dilation, heads_per_step):
    rows = dilation * DIL_BLOCK
    nb = seq // rows
    cur = lambda b, n, hg: (b * nb + n, hg)
    prev = lambda b, n, hg: (b * nb + jnp.maximum(n - 1, 0), hg)
    blk = (rows, heads_per_step * DIL_HEAD_DIM)
    return pl.pallas_call(
        functools.partial(_dilated_kernel, span=window // dilation, dilation=dilation, heads=heads_per_step),
        grid=(batch, nb, DIL_SLOTS // heads_per_step),
        in_specs=[pl.BlockSpec(blk, cur), pl.BlockSpec(blk, prev), pl.BlockSpec(blk, cur),
                  pl.BlockSpec(blk, prev), pl.BlockSpec(blk, cur)],
        out_specs=[pl.BlockSpec(blk, cur), pl.BlockSpec(blk, cur)],
        out_shape=[jax.ShapeDtypeStruct((batch * seq, DIL_KV_WIDTH), F32)] * 2,
        compiler_params=_params(3),
        name="dilated_attention",
    )(q, k, k, v, v)


def _post_kernel(*refs, n_groups):
    x_ref, qm_ref, km_ref, vm_ref, wmix_ref, wmem_ref, g_ref, b_ref = refs[:8]
    mix_refs = refs[8:-1]
    o_ref = refs[-1]
    if n_groups == 0:
        mix = mix_refs[0][...]
    else:
        lses = [mix_refs[n_groups + g][...] for g in range(n_groups)]
        mx = functools.reduce(jnp.maximum, lses)
        ws = [jnp.exp(l - mx) for l in lses]
        den = functools.reduce(lambda a, c: a + c, ws)
        mix = functools.reduce(lambda a, c: a + c, [w * mix_refs[g][...] for g, w in enumerate(ws)]) / den
    qm = qm_ref[...]
    km = km_ref[...].astype(BF16)
    vm = vm_ref[...].astype(BF16)
    lane = lax.broadcasted_iota(jnp.int32, qm.shape, 1)
    mo = jnp.zeros(qm.shape, F32)
    for h in range(MEM_HEADS):
        head = (lane >> MEM_HEAD_SHIFT) == h
        s = _dot_nt(jnp.where(head, qm, 0.0).astype(BF16), km) * (MEM_HEAD_DIM ** -0.5)
        m = jnp.max(s, axis=-1, keepdims=True)
        p = jnp.exp(s - m)
        p = p / jnp.sum(p, axis=-1, keepdims=True)
        mo = mo + jnp.where(head, _dot(p.astype(BF16), vm), 0.0)
    y = _dot(mix.astype(BF16), wmix_ref[...]) + _dot(mo.astype(BF16), wmem_ref[...])
    o_ref[...] = _layer_norm(DN_ALPHA * x_ref[...] + y, g_ref[...], b_ref[...])


def _post_mixer(x, qm, mkv, w_mix, w_mem, g, b, mixes, *, seq, n_groups, tm=256):
    t = x.shape[0]
    per_batch = seq // tm
    row = lambda i: (i, 0)
    const = lambda i: (0, 0)
    mix_w = w_mix.shape[0]
    in_specs = [pl.BlockSpec((tm, D_MODEL), row), pl.BlockSpec((tm, MEM_WIDTH), row),
                pl.BlockSpec((MEM_LEN, MEM_WIDTH), lambda i: (i // per_batch, 0)),
                pl.BlockSpec((MEM_LEN, MEM_WIDTH), lambda i: (i // per_batch, 1)),
                pl.BlockSpec((mix_w, D_MODEL), const), pl.BlockSpec((MEM_WIDTH, D_MODEL), const),
                pl.BlockSpec((1, D_MODEL), const), pl.BlockSpec((1, D_MODEL), const)]
    in_specs += [pl.BlockSpec((tm, mix_w), row) for _ in mixes]
    return pl.pallas_call(
        functools.partial(_post_kernel, n_groups=n_groups),
        grid=(t // tm,),
        in_specs=in_specs,
        out_specs=pl.BlockSpec((tm, D_MODEL), row),
        out_shape=jax.ShapeDtypeStruct((t, D_MODEL), F32),
        compiler_params=_params(1),
        name="post_mixer",
    )(x, qm, mkv, mkv, w_mix, w_mem, g, b, *mixes)


def _top_rows(s, n_rows, val_ref, idx_ref):
    iota = lax.broadcasted_iota(jnp.int32, s.shape, 0).astype(F32)
    for r in range(PEER_TOPK):
        m = jnp.max(s, axis=0, keepdims=True)
        am = jnp.min(jnp.where(s == m, iota, float(n_rows)), axis=0, keepdims=True)
        val_ref[r:r + 1, :] = m
        idx_ref[r:r + 1, :] = am
        s = jnp.where(iota == am, -jnp.inf, s)


PAIR_COUNTS = tuple(PEER_TOPK // (a + 1) for a in range(PEER_TOPK))
PAIR_OFFSETS = tuple(sum(PAIR_COUNTS[:a]) for a in range(PEER_TOPK))
PAIR_ROWS = -(-sum(PAIR_COUNTS) // 8) * 8


def _route_kernel(x_ref, wq_ref, keys_ref, idx_ref, gate_ref, q_ref, tv_ref, ti_ref, bv_ref, bi_ref,
                  cand_ref, e_ref, gt_ref):
    tm = x_ref.shape[0]
    q_ref[...] = _dot(x_ref[...].astype(BF16), wq_ref[...]).astype(BF16)
    n_pairs = sum(PAIR_COUNTS)
    cand_ref[n_pairs:, :] = jnp.full((PAIR_ROWS - n_pairs, tm), -jnp.inf, F32)
    for h in range(PEER_HEADS):
        for p in range(2):
            c0 = (h * 2 + p) * PEER_HALF
            s = _dot_nt(keys_ref[p], q_ref[:, c0:c0 + PEER_HALF])
            _top_rows(s, PEER_N_KEYS, tv_ref.at[p], ti_ref.at[p])
        for a in range(PEER_TOPK):
            cand_ref[PAIR_OFFSETS[a]:PAIR_OFFSETS[a] + PAIR_COUNTS[a], :] = (
                tv_ref[0, a:a + 1, :] + tv_ref[1, 0:PAIR_COUNTS[a], :])
        _top_rows(cand_ref[...], PAIR_ROWS, bv_ref, bi_ref)
        best_s = bv_ref[...]
        best_r = bi_ref[...].astype(jnp.int32)
        ia = jnp.zeros(best_r.shape, F32)
        ib = jnp.zeros(best_r.shape, F32)
        row0 = jnp.zeros(best_r.shape, jnp.int32)
        for a in range(PEER_TOPK):
            at_least = best_r >= PAIR_OFFSETS[a]
            ia = jnp.where(at_least, ti_ref[0, a:a + 1, :], ia)
            row0 = jnp.where(at_least, PAIR_OFFSETS[a], row0)
        jb = best_r - row0
        for b in range(PEER_TOPK):
            ib = jnp.where(jb == b, ti_ref[1, b:b + 1, :], ib)
        pe = jnp.exp(best_s - best_s[0:1, :])
        rs = slice(h * PEER_TOPK, (h + 1) * PEER_TOPK)
        e_ref[rs, :] = (ia.astype(jnp.int32) * PEER_N_KEYS + ib.astype(jnp.int32)) * PEER_ROW_WORDS
        gt_ref[rs, :] = pe / jnp.sum(pe, axis=0, keepdims=True)
    idx_ref[...] = pltpu.bitcast(pltpu.bitcast(e_ref[...], F32).T, jnp.int32)
    gate_ref[...] = gt_ref[...].T


def _peer_route(x, wq, keys, *, tm=256):
    t = x.shape[0]
    nq = wq.shape[1]
    return pl.pallas_call(
        _route_kernel,
        grid=(t // tm,),
        in_specs=[pl.BlockSpec((tm, D_MODEL), lambda i: (i, 0)),
                  pl.BlockSpec((D_MODEL, nq), lambda i: (0, 0)),
                  pl.BlockSpec((2, PEER_N_KEYS, PEER_HALF), lambda i: (0, 0, 0))],
        out_specs=[pl.BlockSpec((tm, PEER_SEL), lambda i: (i, 0))] * 2,
        out_shape=[jax.ShapeDtypeStruct((t, PEER_SEL), jnp.int32),
                   jax.ShapeDtypeStruct((t, PEER_SEL), F32)],
        scratch_shapes=[pltpu.VMEM((tm, nq), BF16),
                        pltpu.VMEM((2, PEER_TOPK, tm), F32), pltpu.VMEM((2, PEER_TOPK, tm), F32),
                        pltpu.VMEM((PEER_TOPK, tm), F32), pltpu.VMEM((PEER_TOPK, tm), F32),
                        pltpu.VMEM((PAIR_ROWS, tm), F32),
                        pltpu.VMEM((PEER_SEL, tm), jnp.int32), pltpu.VMEM((PEER_SEL, tm), F32)],
        compiler_params=_params(1),
        name="peer_route",
    )(x, wq, keys)


def _pack_kernel(w_ref, o_ref):
    te = w_ref.shape[0]
    for s in range(PEER_ROW_WORDS):
        lo = w_ref[:, 2 * s * LANES:(2 * s + 1) * LANES].astype(BF16).astype(F32)
        hi = w_ref[:, (2 * s + 1) * LANES:(2 * s + 2) * LANES].astype(BF16).astype(F32)
        word = (pltpu.bitcast(lo, jnp.uint32) >> 16) | (pltpu.bitcast(hi, jnp.uint32) & jnp.uint32(0xFFFF0000))
        o_ref[pl.ds(s, te, stride=PEER_ROW_WORDS), :] = word


def _pack_table(w, *, te=512):
    e = w.shape[0]
    return pl.pallas_call(
        _pack_kernel,
        grid=(e // te,),
        in_specs=[pl.BlockSpec((te, D_MODEL), lambda i: (i, 0))],
        out_specs=pl.BlockSpec((te * PEER_ROW_WORDS, LANES), lambda i: (i, 0)),
        out_shape=jax.ShapeDtypeStruct((e * PEER_ROW_WORDS, LANES), jnp.uint32),
        compiler_params=_params(1),
        name="pack_table",
    )(w)


def _gather_rows(idx_ref, t, tbl_ref, w_ref):
    for j in range(PEER_SEL):
        i = pl.multiple_of(idx_ref[t, j], PEER_ROW_WORDS)
        w_ref[PEER_ROW_WORDS * j:PEER_ROW_WORDS * (j + 1), :] = tbl_ref[pl.ds(i, PEER_ROW_WORDS), :]


def _staging_buffers():
    return [pltpu.VMEM((PEER_SEL * PEER_ROW_WORDS, LANES), jnp.uint32) for _ in range(PEER_STAGES)]


def _token_pipeline(idx_ref, tbl_ref, w_refs, compute, tb, unroll):
    n = PEER_STAGES
    assert len(w_refs) == n and unroll % n == 0 and tb % unroll == 0
    for s in range(n - 1):
        _gather_rows(idx_ref, s, tbl_ref, w_refs[s])

    def body(i, carry):
        t0 = unroll * i
        for s in range(unroll):
            compute(t0 + s, w_refs[s % n])
            ahead = jnp.minimum(t0 + s + n - 1, tb - 1)
            _gather_rows(idx_ref, ahead, tbl_ref, w_refs[(s + n - 1) % n])
        return carry

    lax.fori_loop(0, tb // unroll, body, 0)


def _peer_u_kernel(idx_ref, x_ref, gate_ref, tbl_ref, a_ref, hb_ref, x8_ref, *w_refs, tb):
    width = PEER_SEL * PEER_CHUNKS
    for c in range(PEER_CHUNKS):
        x8_ref[pl.ds(c, tb, stride=PEER_CHUNKS), :] = x_ref[:, c * LANES:(c + 1) * LANES]
    lane = lax.broadcasted_iota(jnp.int32, (2 * PEER_CHUNKS, width), 1)
    row = lax.broadcasted_iota(jnp.int32, (2 * PEER_CHUNKS, width), 0)
    diag = (lane & (PEER_CHUNKS - 1)) == (row & (PEER_CHUNKS - 1))

    def compute(t, w_ref):
        wb = pltpu.bitcast(w_ref[...], BF16)
        x8 = x8_ref[pl.ds(pl.multiple_of(t * PEER_CHUNKS, PEER_CHUNKS), PEER_CHUNKS), :]
        xh = x8.astype(BF16)
        xl = (x8 - xh.astype(F32)).astype(BF16)
        r = _dot_nt(jnp.concatenate([xh, xl], axis=0), wb)
        hb_ref[pl.ds(t, 1), :] = jnp.sum(jnp.where(diag, r, 0.0), axis=0, keepdims=True)

    _token_pipeline(idx_ref, tbl_ref, w_refs, compute, tb, PEER_HIDDEN_BODY_TOKENS)
    gi = lax.broadcasted_iota(jnp.int32, (width, PEER_SEL), 0)
    gj = lax.broadcasted_iota(jnp.int32, (width, PEER_SEL), 1)
    group_sum = ((gi >> PEER_CHUNK_SHIFT) == gj).astype(BF16)
    h = jnp.zeros((tb, PEER_SEL), F32)
    for part in _split3(hb_ref[...]):
        h = h + _dot(part, group_sum)
    a_ref[...] = 0.5 * h * (1.0 + lax.erf(h * (2.0 ** -0.5))) * gate_ref[...]


def _peer_hidden(idx, x, gate, tbl, *, tb=64):
    t = idx.shape[0]
    return pl.pallas_call(
        functools.partial(_peer_u_kernel, tb=tb),
        grid=(t // tb,),
        in_specs=[pl.BlockSpec((tb, PEER_SEL), lambda i: (i, 0), memory_space=pltpu.SMEM),
                  pl.BlockSpec((tb, D_MODEL), lambda i: (i, 0)),
                  pl.BlockSpec((tb, PEER_SEL), lambda i: (i, 0)),
                  pl.BlockSpec(memory_space=pltpu.VMEM)],
        out_specs=pl.BlockSpec((tb, PEER_SEL), lambda i: (i, 0)),
        out_shape=jax.ShapeDtypeStruct((t, PEER_SEL), F32),
        scratch_shapes=[pltpu.VMEM((tb, PEER_SEL * PEER_CHUNKS), F32),
                        pltpu.VMEM((tb * PEER_CHUNKS, LANES), F32)] + _staging_buffers(),
        compiler_params=_params(1),
        name="peer_hidden",
    )(idx, x, gate, tbl)


def _peer_v_kernel(idx_ref, a_ref, tbl_ref, f8_ref, ah_ref, al_ref, *w_refs, tb):
    width = PEER_SEL * PEER_CHUNKS
    ei = lax.broadcasted_iota(jnp.int32, (PEER_SEL, width), 0)
    ej = lax.broadcasted_iota(jnp.int32, (PEER_SEL, width), 1)
    expand = (ei == (ej >> PEER_CHUNK_SHIFT)).astype(BF16)
    a = a_ref[...]
    ah = a.astype(BF16)
    al = (a - ah.astype(F32)).astype(BF16)
    ah_ref[...] = _dot(ah, expand)
    al_ref[...] = _dot(al, expand)
    lane = lax.broadcasted_iota(jnp.int32, (PEER_CHUNKS, width), 1)
    row = lax.broadcasted_iota(jnp.int32, (PEER_CHUNKS, width), 0)
    diag = (lane & (PEER_CHUNKS - 1)) == row

    def compute(t, w_ref):
        wb = pltpu.bitcast(w_ref[...], BF16)
        a8h = jnp.where(diag, ah_ref[pl.ds(t, 1), :], 0.0).astype(BF16)
        a8l = jnp.where(diag, al_ref[pl.ds(t, 1), :], 0.0).astype(BF16)
        r = _dot(jnp.concatenate([a8h, a8l], axis=0), wb)
        f8_ref[pl.ds(pl.multiple_of(t * PEER_CHUNKS, PEER_CHUNKS), PEER_CHUNKS), :] = (
            r[0:PEER_CHUNKS] + r[PEER_CHUNKS:2 * PEER_CHUNKS])

    _token_pipeline(idx_ref, tbl_ref, w_refs, compute, tb, PEER_OUTPUT_BODY_TOKENS)


def _peer_output(idx, a, tbl, *, tb=64):
    t = idx.shape[0]
    return pl.pallas_call(
        functools.partial(_peer_v_kernel, tb=tb),
        grid=(t // tb,),
        in_specs=[pl.BlockSpec((tb, PEER_SEL), lambda i: (i, 0), memory_space=pltpu.SMEM),
                  pl.BlockSpec((tb, PEER_SEL), lambda i: (i, 0)),
                  pl.BlockSpec(memory_space=pltpu.VMEM)],
        out_specs=pl.BlockSpec((tb * PEER_CHUNKS, LANES), lambda i: (i, 0)),
        out_shape=jax.ShapeDtypeStruct((t * PEER_CHUNKS, LANES), F32),
        scratch_shapes=[pltpu.VMEM((tb, PEER_SEL * PEER_CHUNKS), F32),
                        pltpu.VMEM((tb, PEER_SEL * PEER_CHUNKS), F32)] + _staging_buffers(),
        compiler_params=_params(1),
        name="peer_output",
    )(idx, a, tbl)


def _add_norm_kernel(x_ref, f8_ref, g_ref, b_ref, o_ref):
    tm = x_ref.shape[0]
    f = jnp.concatenate([f8_ref[pl.ds(c, tm, stride=PEER_CHUNKS), :] for c in range(PEER_CHUNKS)], axis=1)
    o_ref[...] = _layer_norm(DN_ALPHA * x_ref[...] + f, g_ref[...], b_ref[...])


def _add_norm(x, f8, g, b, *, tm=512):
    t = x.shape[0]
    row = lambda i: (i, 0)
    const = lambda i: (0, 0)
    return pl.pallas_call(
        _add_norm_kernel,
        grid=(t // tm,),
        in_specs=[pl.BlockSpec((tm, D_MODEL), row), pl.BlockSpec((tm * PEER_CHUNKS, LANES), row),
                  pl.BlockSpec((1, D_MODEL), const), pl.BlockSpec((1, D_MODEL), const)],
        out_specs=pl.BlockSpec((tm, D_MODEL), row),
        out_shape=jax.ShapeDtypeStruct((t, D_MODEL), F32),
        compiler_params=_params(1),
        name="add_norm",
    )(x, f8, g, b)


def _peer(x, w_q, sub_keys, u, v, g, b):
    idx, gate = _peer_route(x, w_q.astype(BF16), sub_keys.astype(BF16))
    a = _peer_hidden(idx, x, gate, _pack_table(u))
    f8 = _peer_output(idx, a, _pack_table(v))
    return _add_norm(x, f8, g, b)


def _pad_heads(w, heads, dim, dim_pad, axis=-1):
    axis = axis % w.ndim
    shape = w.shape[:axis] + (heads, dim) + w.shape[axis + 1:]
    pad = [(0, 0)] * (w.ndim + 1)
    pad[axis + 1] = (0, dim_pad - dim)
    out = jnp.pad(w.reshape(shape), pad)
    return out.reshape(w.shape[:axis] + (heads * dim_pad,) + w.shape[axis + 1:])


def kernel(x, mem, a_w_in, a_w_gate2, a_b_gate, a_norm_g, b_w_in, shared_w_kv, w_mem_kv, w_out, ln_mix_g,
           ln_mix_b, ln_ffn_g, ln_ffn_b, peer_w_q, peer_sub_keys, peer_u, peer_v):
    batch, seq, d = x.shape
    t = batch * seq
    xt = x.reshape(t, d)
    mem2 = mem.reshape(batch * MEM_LEN, d)
    row = lambda a: a.reshape(1, -1)
    qk_w = GLA_HEADS * GLA_DK
    v_w = GLA_HEADS * GLA_DV
    kd = None
    vd = None
    for l in range(DEPTH):
        (mkv,) = _matmul(mem2, w_mem_kv[l].astype(BF16), [2 * MEM_WIDTH], [F32])
        w_mem = w_out[l, v_w:].astype(BF16)
        if l < N_A_LAYERS:
            w = a_w_in[l]
            q0, k0, v0, r0 = 0, qk_w, 2 * qk_w, 2 * qk_w + v_w
            g0 = r0 + v_w
            m0 = g0 + GLA_GATE_RANK
            w_in = jnp.concatenate([
                _pad_heads(w[:, q0:k0], GLA_HEADS, GLA_DK, GLA_DK_PAD),
                _pad_heads(w[:, k0:v0], GLA_HEADS, GLA_DK, GLA_DK_PAD),
                _pad_heads(w[:, v0:r0], GLA_HEADS, GLA_DV, GLA_DV_PAD),
                _pad_heads(w[:, r0:g0], GLA_HEADS, GLA_DV, GLA_DV_PAD),
                w[:, m0:m0 + MEM_WIDTH],
                jnp.pad(w[:, g0:m0], ((0, 0), (0, LANES - GLA_GATE_RANK)))], axis=1).astype(BF16)
            qkw, vw = GLA_HEADS * GLA_DK_PAD, GLA_HEADS * GLA_DV_PAD
            q, k, v, r, qm, gin = _matmul(xt, w_in, [qkw, qkw, vw, vw, MEM_WIDTH, LANES], [F32] * 6)
            wg = jnp.pad(_pad_heads(a_w_gate2[l], GLA_HEADS, GLA_DK, GLA_DK_PAD),
                         ((0, LANES - GLA_GATE_RANK), (0, 0))).astype(BF16)
            bg = row(_pad_heads(a_b_gate[l], GLA_HEADS, GLA_DK, GLA_DK_PAD))
            ng = row(_pad_heads(a_norm_g[l], GLA_HEADS, GLA_DV, GLA_DV_PAD))
            mix = _gla(q, k, v, r, gin, wg, bg, ng, batch=batch, seq=seq)
            w_mix = _pad_heads(w_out[l, :v_w], GLA_HEADS, GLA_DV, GLA_DV_PAD, axis=0).astype(BF16)
            mixes, n_groups = [mix], 0
        else:
            w_in = b_w_in[l - N_A_LAYERS].astype(BF16)
            n_g = len(DIL_PAIRS)
            outs = _matmul(xt, w_in, [DIL_KV_WIDTH] * n_g + [MEM_WIDTH], [F32] * (n_g + 1))
            qm = outs[n_g]
            res = [_dilated_group(outs[g], kd, vd, batch=batch, seq=seq, window=wd, dilation=dl,
                                  heads_per_step=_dilated_heads_per_step(dl))
                   for g, (wd, dl) in enumerate(DIL_PAIRS)]
            mixes = [o for o, _ in res] + [s for _, s in res]
            n_groups = n_g
            w_mix = w_out[l, :DIL_KV_WIDTH].astype(BF16)
        xt = _post_mixer(xt, qm, mkv, w_mix, w_mem, row(ln_mix_g[l]), row(ln_mix_b[l]), mixes,
                         seq=seq, n_groups=n_groups)
        xt = _peer(xt, peer_w_q[l], peer_sub_keys[l], peer_u[l], peer_v[l], row(ln_ffn_g[l]), row(ln_ffn_b[l]))
        if l == N_A_LAYERS - 1:
            kd, vd = _matmul(xt, shared_w_kv.astype(BF16), [DIL_KV_WIDTH] * 2, [F32] * 2)
    return xt.reshape(batch, seq, d)
```

```python
import functools

import jax
import jax.numpy as jnp
from jax import lax
from jax.experimental import pallas as pl
from jax.experimental.pallas import tpu as pltpu

D_MODEL = 1024
DEPTH = 2
N_A_LAYERS = 1
DN_ALPHA = (2.0 * DEPTH) ** 0.25
LN_EPS = 1e-5
HEAD_NORM_EPS = 1e-6

MEM_LEN = 256
MEM_HEADS = 4
MEM_HEAD_DIM = 64
MEM_HEAD_SHIFT = 6
MEM_WIDTH = 256

GLA_HEADS = 4
GLA_DK = 96
GLA_DV = 192
GLA_DK_PAD = 128
GLA_DV_PAD = 256
GLA_GATE_RANK = 16
GLA_TAU = 16.0
GLA_CHUNK = 64
GLA_CHUNK_SHIFT = 6

DIL_PAIRS = ((128, 1), (512, 4), (2048, 16))
DIL_SLOTS = 6
DIL_HEAD_DIM = 128
DIL_BLOCK = 128
DIL_KV_WIDTH = DIL_SLOTS * DIL_HEAD_DIM

PEER_N_KEYS = 128
PEER_HEADS = 8
PEER_TOPK = 16
PEER_HALF = 128
PEER_SEL = PEER_HEADS * PEER_TOPK
PEER_ROW_WORDS = 4
PEER_CHUNKS = D_MODEL // 128
PEER_CHUNK_SHIFT = 3
PEER_STAGES = 4
PEER_SUB_TOKENS = 32
PEER_TOPK_SHIFT = 4

LANES = 128
VMEM_LIMIT_BYTES = 56 * 1024 * 1024

BF16 = jnp.bfloat16
F32 = jnp.float32


def _params(n_grid_dims):
    return pltpu.CompilerParams(dimension_semantics=("arbitrary",) * n_grid_dims,
                                vmem_limit_bytes=VMEM_LIMIT_BYTES)


def _dot(a, b):
    return jnp.dot(a, b, preferred_element_type=F32)


def _dot_nt(a, b):
    return lax.dot_general(a, b, (((1,), (1,)), ((), ())), preferred_element_type=F32)


def _dot_tn(a, b):
    return lax.dot_general(a, b, (((0,), (0,)), ((), ())), preferred_element_type=F32)


def _split3(v):
    hi = v.astype(BF16)
    r1 = v - hi.astype(F32)
    mid = r1.astype(BF16)
    lo = (r1 - mid.astype(F32)).astype(BF16)
    return hi, mid, lo


def _layer_norm(z, g, b):
    mu = jnp.mean(z, axis=-1, keepdims=True)
    zc = z - mu
    var = jnp.mean(zc * zc, axis=-1, keepdims=True)
    return zc * lax.rsqrt(var + LN_EPS) * g + b


def _mm_kernel(x_ref, w_ref, *o_refs, splits):
    y = _dot(x_ref[...].astype(BF16), w_ref[...])
    off = 0
    for o_ref, width in zip(o_refs, splits):
        o_ref[...] = y[:, off:off + width].astype(o_ref.dtype)
        off += width


def _matmul(x, w, splits, dtypes, *, tm=256):
    t, k = x.shape
    n = w.shape[1]
    assert sum(splits) == n and t % tm == 0
    return pl.pallas_call(
        functools.partial(_mm_kernel, splits=tuple(splits)),
        grid=(t // tm,),
        in_specs=[pl.BlockSpec((tm, k), lambda i: (i, 0)),
                  pl.BlockSpec((k, n), lambda i: (0, 0))],
        out_specs=[pl.BlockSpec((tm, s), lambda i: (i, 0)) for s in splits],
        out_shape=[jax.ShapeDtypeStruct((t, s), d) for s, d in zip(splits, dtypes)],
        compiler_params=_params(1),
        name="matmul",
    )(x, w)


def _gla_kernel(q_ref, k_ref, v_ref, r_ref, g_ref, wg_ref, bg_ref, ng_ref, o_ref, state_ref, *, rows):
    nchunk = rows // GLA_CHUNK

    @pl.when(pl.program_id(1) == 0)
    def _():
        state_ref[...] = jnp.zeros_like(state_ref)

    g_pre = _dot(g_ref[...].astype(BF16), wg_ref[...]) + bg_ref[...]
    log_a = (jnp.minimum(g_pre, 0.0) - jnp.log1p(jnp.exp(-jnp.abs(g_pre)))) / GLA_TAU

    ri = lax.broadcasted_iota(jnp.int32, (rows, rows), 0)
    ci = lax.broadcasted_iota(jnp.int32, (rows, rows), 1)
    same_chunk_causal = (ci <= ri) & ((ri >> GLA_CHUNK_SHIFT) == (ci >> GLA_CHUNK_SHIFT))
    tril = same_chunk_causal.astype(BF16)
    b = jnp.zeros_like(log_a)
    for part in _split3(log_a):
        b = b + _dot(tril, part)
    b_last = jnp.concatenate(
        [jnp.broadcast_to(b[(c + 1) * GLA_CHUNK - 1:(c + 1) * GLA_CHUNK, :], (GLA_CHUNK, b.shape[1]))
         for c in range(nchunk)], axis=0)

    q = q_ref[...] * (GLA_DK ** -0.5)
    k = k_ref[...]
    q_t = (q * jnp.exp(b)).astype(BF16)
    k_t = (k * jnp.exp(-b)).astype(BF16)
    k_end = (k * jnp.exp(b_last - b)).astype(BF16)
    decay = jnp.exp(b_last)
    v = v_ref[...].astype(BF16)
    r = r_ref[...]
    ng = ng_ref[...]

    for h in range(GLA_HEADS):
        ks = slice(h * GLA_DK_PAD, (h + 1) * GLA_DK_PAD)
        vs = slice(h * GLA_DV_PAD, (h + 1) * GLA_DV_PAD)
        attn = jnp.where(same_chunk_causal, _dot_nt(q_t[:, ks], k_t[:, ks]), 0.0).astype(BF16)
        o_intra = _dot(attn, v[:, vs])
        st = state_ref[h]
        o_inter = []
        for c in range(nchunk):
            rs = slice(c * GLA_CHUNK, (c + 1) * GLA_CHUNK)
            o_inter.append(_dot_nt(q_t[rs, ks], st.astype(BF16)))
            st = st * decay[c * GLA_CHUNK:c * GLA_CHUNK + 1, ks] + _dot_tn(v[rs, vs], k_end[rs, ks])
        state_ref[h] = st
        o = o_intra + jnp.concatenate(o_inter, axis=0)
        ms = jnp.sum(o * o, axis=-1, keepdims=True) / GLA_DV
        o = o * lax.rsqrt(ms + HEAD_NORM_EPS) * ng[:, vs]
        rh = r[:, vs]
        o_ref[:, vs] = (rh * jax.nn.sigmoid(rh) * o).astype(o_ref.dtype)


def _gla(q, k, v, r, g, wg, bg, ng, *, batch, seq, rows=256):
    nblk = seq // rows
    row_map = lambda b, i: (b * nblk + i, 0)
    const = lambda b, i: (0, 0)
    qk_w, v_w = GLA_HEADS * GLA_DK_PAD, GLA_HEADS * GLA_DV_PAD
    return pl.pallas_call(
        functools.partial(_gla_kernel, rows=rows),
        grid=(batch, nblk),
        in_specs=[pl.BlockSpec((rows, qk_w), row_map), pl.BlockSpec((rows, qk_w), row_map),
                  pl.BlockSpec((rows, v_w), row_map), pl.BlockSpec((rows, v_w), row_map),
                  pl.BlockSpec((rows, LANES), row_map),
                  pl.BlockSpec((LANES, qk_w), const), pl.BlockSpec((1, qk_w), const),
                  pl.BlockSpec((1, v_w), const)],
        out_specs=pl.BlockSpec((rows, v_w), row_map),
        out_shape=jax.ShapeDtypeStruct((batch * seq, v_w), BF16),
        scratch_shapes=[pltpu.VMEM((GLA_HEADS, GLA_DV_PAD, GLA_DK_PAD), F32)],
        compiler_params=_params(2),
        name="gla",
    )(q, k, v, r, g, wg, bg, ng)


def _dilated_kernel(q_ref, kp_ref, kc_ref, vp_ref, vc_ref, o_ref, lse_ref, *, span, dilation, heads):
    first_key = jnp.where(pl.program_id(1) == 0, DIL_BLOCK, 0)
    qi = lax.broadcasted_iota(jnp.int32, (DIL_BLOCK, 2 * DIL_BLOCK), 0)
    kj = lax.broadcasted_iota(jnp.int32, (DIL_BLOCK, 2 * DIL_BLOCK), 1)
    rel = DIL_BLOCK + qi - kj
    mask = (rel >= 0) & (rel <= span) & (kj >= first_key)
    scale = DIL_HEAD_DIM ** -0.5
    for r in range(dilation):
        rows = pl.ds(r, DIL_BLOCK, stride=dilation) if dilation > 1 else slice(None)
        for h in range(heads):
            hs = slice(h * DIL_HEAD_DIM, (h + 1) * DIL_HEAD_DIM)
            q = q_ref[rows, hs].astype(BF16)
            kcat = jnp.concatenate([kp_ref[rows, hs], kc_ref[rows, hs]], axis=0).astype(BF16)
            vcat = jnp.concatenate([vp_ref[rows, hs], vc_ref[rows, hs]], axis=0).astype(BF16)
            s = jnp.where(mask, _dot_nt(q, kcat) * scale, -jnp.inf)
            m = jnp.max(s, axis=-1, keepdims=True)
            p = jnp.exp(s - m)
            l = jnp.sum(p, axis=-1, keepdims=True)
            o_ref[rows, hs] = _dot(p.astype(BF16), vcat) / l
            lse_ref[rows, hs] = jnp.broadcast_to(m + jnp.log(l), (DIL_BLOCK, DIL_HEAD_DIM))


def _dilated_heads_per_step(dilation):
    return DIL_SLOTS if dilation == 1 else 1


def _dilated_group(q, k, v, *, batch, seq, window, dilation, heads_per_step):
    rows = dilation * DIL_BLOCK
    nb = seq // rows
    cur = lambda b, n, hg: (b * nb + n, hg)
    prev = lambda b, n, hg: (b * nb + jnp.maximum(n - 1, 0), hg)
    blk = (rows, heads_per_step * DIL_HEAD_DIM)
    return pl.pallas_call(
        functools.partial(_dilated_kernel, span=window // dilation, dilation=dilation, heads=heads_per_step),
        grid=(batch, nb, DIL_SLOTS // heads_per_step),
        in_specs=[pl.BlockSpec(blk, cur), pl.BlockSpec(blk, prev), pl.BlockSpec(blk, cur),
                  pl.BlockSpec(blk, prev), pl.BlockSpec(blk, cur)],
        out_specs=[pl.BlockSpec(blk, cur), pl.BlockSpec(blk, cur)],
        out_shape=[jax.ShapeDtypeStruct((batch * seq, DIL_KV_WIDTH), F32)] * 2,
        compiler_params=_params(3),
        name="dilated_attention",
    )(q, k, k, v, v)


def _post_kernel(*refs, n_groups):
    x_ref, qm_ref, km_ref, vm_ref, wmix_ref, wmem_ref, g_ref, b_ref = refs[:8]
    mix_refs = refs[8:-1]
    o_ref = refs[-1]
    if n_groups == 0:
        mix = mix_refs[0][...]
    else:
        lses = [mix_refs[n_groups + g][...] for g in range(n_groups)]
        mx = functools.reduce(jnp.maximum, lses)
        ws = [jnp.exp(l - mx) for l in lses]
        den = functools.reduce(lambda a, c: a + c, ws)
        mix = functools.reduce(lambda a, c: a + c, [w * mix_refs[g][...] for g, w in enumerate(ws)]) / den
    qm = qm_ref[...]
    km = km_ref[...].astype(BF16)
    vm = vm_ref[...].astype(BF16)
    lane = lax.broadcasted_iota(jnp.int32, qm.shape, 1)
    mo = jnp.zeros(qm.shape, F32)
    for h in range(MEM_HEADS):
        head = (lane >> MEM_HEAD_SHIFT) == h
        s = _dot_nt(jnp.where(head, qm, 0.0).astype(BF16), km) * (MEM_HEAD_DIM ** -0.5)
        m = jnp.max(s, axis=-1, keepdims=True)
        p = jnp.exp(s - m)
        p = p / jnp.sum(p, axis=-1, keepdims=True)
        mo = mo + jnp.where(head, _dot(p.astype(BF16), vm), 0.0)
    y = _dot(mix.astype(BF16), wmix_ref[...]) + _dot(mo.astype(BF16), wmem_ref[...])
    o_ref[...] = _layer_norm(DN_ALPHA * x_ref[...] + y, g_ref[...], b_ref[...])


def _post_mixer(x, qm, mkv, w_mix, w_mem, g, b, mixes, *, seq, n_groups, tm=256):
    t = x.shape[0]
    per_batch = seq // tm
    row = lambda i: (i, 0)
    const = lambda i: (0, 0)
    mix_w = w_mix.shape[0]
    in_specs = [pl.BlockSpec((tm, D_MODEL), row), pl.BlockSpec((tm, MEM_WIDTH), row),
                pl.BlockSpec((MEM_LEN, MEM_WIDTH), lambda i: (i // per_batch, 0)),
                pl.BlockSpec((MEM_LEN, MEM_WIDTH), lambda i: (i // per_batch, 1)),
                pl.BlockSpec((mix_w, D_MODEL), const), pl.BlockSpec((MEM_WIDTH, D_MODEL), const),
                pl.BlockSpec((1, D_MODEL), const), pl.BlockSpec((1, D_MODEL), const)]
    in_specs += [pl.BlockSpec((tm, mix_w), row) for _ in mixes]
    return pl.pallas_call(
        functools.partial(_post_kernel, n_groups=n_groups),
        grid=(t // tm,),
        in_specs=in_specs,
        out_specs=pl.BlockSpec((tm, D_MODEL), row),
        out_shape=jax.ShapeDtypeStruct((t, D_MODEL), F32),
        compiler_params=_params(1),
        name="post_mixer",
    )(x, qm, mkv, mkv, w_mix, w_mem, g, b, *mixes)


def _top_rows(s, n_rows, val_ref, idx_ref):
    iota = lax.broadcasted_iota(jnp.int32, s.shape, 0).astype(F32)
    for r in range(PEER_TOPK):
        m = jnp.max(s, axis=0, keepdims=True)
        am = jnp.min(jnp.where(s == m, iota, float(n_rows)), axis=0, keepdims=True)
        val_ref[r:r + 1, :] = m
        idx_ref[r:r + 1, :] = am
        s = jnp.where(iota == am, -jnp.inf, s)


PAIR_COUNTS = tuple(PEER_TOPK // (a + 1) for a in range(PEER_TOPK))
PAIR_OFFSETS = tuple(sum(PAIR_COUNTS[:a]) for a in range(PEER_TOPK))
PAIR_ROWS = -(-sum(PAIR_COUNTS) // 8) * 8


def _route_kernel(x_ref, wq_ref, keys_ref, idx_ref, gate_ref, q_ref, tv_ref, ti_ref, bv_ref, bi_ref,
                  cand_ref, e_ref, gt_ref):
    tm = x_ref.shape[0]
    q_ref[...] = _dot(x_ref[...].astype(BF16), wq_ref[...]).astype(BF16)
    n_pairs = sum(PAIR_COUNTS)
    cand_ref[n_pairs:, :] = jnp.full((PAIR_ROWS - n_pairs, tm), -jnp.inf, F32)
    for h in range(PEER_HEADS):
        for p in range(2):
            c0 = (h * 2 + p) * PEER_HALF
            s = _dot_nt(keys_ref[p], q_ref[:, c0:c0 + PEER_HALF])
            _top_rows(s, PEER_N_KEYS, tv_ref.at[p], ti_ref.at[p])
        for a in range(PEER_TOPK):
            cand_ref[PAIR_OFFSETS[a]:PAIR_OFFSETS[a] + PAIR_COUNTS[a], :] = (
                tv_ref[0, a:a + 1, :] + tv_ref[1, 0:PAIR_COUNTS[a], :])
        _top_rows(cand_ref[...], PAIR_ROWS, bv_ref, bi_ref)
        best_s = bv_ref[...]
        best_r = bi_ref[...].astype(jnp.int32)
        ia = jnp.zeros(best_r.shape, F32)
        ib = jnp.zeros(best_r.shape, F32)
        row0 = jnp.zeros(best_r.shape, jnp.int32)
        for a in range(PEER_TOPK):
            at_least = best_r >= PAIR_OFFSETS[a]
            ia = jnp.where(at_least, ti_ref[0, a:a + 1, :], ia)
            row0 = jnp.where(at_least, PAIR_OFFSETS[a], row0)
        jb = best_r - row0
        for b in range(PEER_TOPK):
            ib = jnp.where(jb == b, ti_ref[1, b:b + 1, :], ib)
        pe = jnp.exp(best_s - best_s[0:1, :])
        rs = slice(h * PEER_TOPK, (h + 1) * PEER_TOPK)
        e_ref[rs, :] = (ia.astype(jnp.int32) * PEER_N_KEYS + ib.astype(jnp.int32)) * PEER_ROW_WORDS
        gt_ref[rs, :] = pe / jnp.sum(pe, axis=0, keepdims=True)
    idx_ref[...] = pltpu.bitcast(pltpu.bitcast(e_ref[...], F32).T, jnp.int32)
    gate_ref[...] = gt_ref[...].T


def _peer_route(x, wq, keys, *, tm=256):
    t = x.shape[0]
    nq = wq.shape[1]
    return pl.pallas_call(
        _route_kernel,
        grid=(t // tm,),
        in_specs=[pl.BlockSpec((tm, D_MODEL), lambda i: (i, 0)),
                  pl.BlockSpec((D_MODEL, nq), lambda i: (0, 0)),
                  pl.BlockSpec((2, PEER_N_KEYS, PEER_HALF), lambda i: (0, 0, 0))],
        out_specs=[pl.BlockSpec((tm, PEER_SEL), lambda i: (i, 0))] * 2,
        out_shape=[jax.ShapeDtypeStruct((t, PEER_SEL), jnp.int32),
                   jax.ShapeDtypeStruct((t, PEER_SEL), F32)],
        scratch_shapes=[pltpu.VMEM((tm, nq), BF16),
                        pltpu.VMEM((2, PEER_TOPK, tm), F32), pltpu.VMEM((2, PEER_TOPK, tm), F32),
                        pltpu.VMEM((PEER_TOPK, tm), F32), pltpu.VMEM((PEER_TOPK, tm), F32),
                        pltpu.VMEM((PAIR_ROWS, tm), F32),
                        pltpu.VMEM((PEER_SEL, tm), jnp.int32), pltpu.VMEM((PEER_SEL, tm), F32)],
        compiler_params=_params(1),
        name="peer_route",
    )(x, wq, keys)


def _pack_kernel(w_ref, o_ref):
    te = w_ref.shape[0]
    for s in range(PEER_ROW_WORDS):
        lo = w_ref[:, 2 * s * LANES:(2 * s + 1) * LANES].astype(BF16).astype(F32)
        hi = w_ref[:, (2 * s + 1) * LANES:(2 * s + 2) * LANES].astype(BF16).astype(F32)
        word = (pltpu.bitcast(lo, jnp.uint32) >> 16) | (pltpu.bitcast(hi, jnp.uint32) & jnp.uint32(0xFFFF0000))
        o_ref[pl.ds(s, te, stride=PEER_ROW_WORDS), :] = word


def _pack_table(w, *, te=512):
    e = w.shape[0]
    return pl.pallas_call(
        _pack_kernel,
        grid=(e // te,),
        in_specs=[pl.BlockSpec((te, D_MODEL), lambda i: (i, 0))],
        out_specs=pl.BlockSpec((te * PEER_ROW_WORDS, LANES), lambda i: (i, 0)),
        out_shape=jax.ShapeDtypeStruct((e * PEER_ROW_WORDS, LANES), jnp.uint32),
        compiler_params=_params(1),
        name="pack_table",
    )(w)


def _gather_rows(idx_ref, t, tbl_ref, w_ref):
    for j in range(PEER_SEL):
        i = pl.multiple_of(idx_ref[t, j], PEER_ROW_WORDS)
        w_ref[PEER_ROW_WORDS * j:PEER_ROW_WORDS * (j + 1), :] = tbl_ref[pl.ds(i, PEER_ROW_WORDS), :]


def _peer_scratch():
    return ([pltpu.VMEM((PEER_SEL * PEER_ROW_WORDS, LANES), jnp.uint32) for _ in range(PEER_STAGES)]
            + [pltpu.SMEM((PEER_SUB_TOKENS, PEER_SEL), jnp.int32)] * 2
            + [pltpu.SemaphoreType.DMA(())] * 2)


def _token_pipeline(idx_hbm, scratch, tbl_ref, compute):
    n = PEER_STAGES
    tbs = PEER_SUB_TOKENS
    assert (2 * tbs) % n == 0
    w_refs, (idx_a, idx_b), (sem_a, sem_b) = scratch[:n], scratch[n:n + 2], scratch[n + 2:n + 4]
    step = pl.program_id(0)
    nxt = jnp.minimum(step + 1, pl.num_programs(0) - 1)

    def copy_a(k):
        return pltpu.make_async_copy(idx_hbm.at[2 * k], idx_a, sem_a)

    def copy_b(k):
        return pltpu.make_async_copy(idx_hbm.at[2 * k + 1], idx_b, sem_b)

    @pl.when(step == 0)
    def _():
        copy_a(0).start()
        copy_a(0).wait()
        for s in range(n - 1):
            _gather_rows(idx_a, s, tbl_ref, w_refs[s])

    copy_b(step).start()
    for g in range(2 * tbs):
        compute(g, w_refs[g % n])
        ahead = g + n - 1
        if ahead == tbs:
            copy_b(step).wait()
        if ahead == 2 * tbs:
            copy_a(nxt).wait()
        if ahead < tbs:
            _gather_rows(idx_a, ahead, tbl_ref, w_refs[ahead % n])
        elif ahead < 2 * tbs:
            _gather_rows(idx_b, ahead - tbs, tbl_ref, w_refs[ahead % n])
        else:
            _gather_rows(idx_a, ahead - 2 * tbs, tbl_ref, w_refs[ahead % n])
        if ahead == tbs - 1:
            copy_a(nxt).start()


def _peer_u_kernel(idx_hbm, x_ref, gate_ref, tbl_ref, a_ref, hb_ref, x8_ref, *scratch):
    tb = x_ref.shape[0]
    width = PEER_SEL * PEER_CHUNKS
    for c in range(PEER_CHUNKS):
        x8_ref[pl.ds(c, tb, stride=PEER_CHUNKS), :] = x_ref[:, c * LANES:(c + 1) * LANES]
    lane = lax.broadcasted_iota(jnp.int32, (2 * PEER_CHUNKS, width), 1)
    row = lax.broadcasted_iota(jnp.int32, (2 * PEER_CHUNKS, width), 0)
    diag = (lane & (PEER_CHUNKS - 1)) == (row & (PEER_CHUNKS - 1))

    def compute(t, w_ref):
        wb = pltpu.bitcast(w_ref[...], BF16)
        x8 = x8_ref[t * PEER_CHUNKS:(t + 1) * PEER_CHUNKS, :]
        xh = x8.astype(BF16)
        xl = (x8 - xh.astype(F32)).astype(BF16)
        r = _dot_nt(jnp.concatenate([xh, xl], axis=0), wb)
        hb_ref[t:t + 1, :] = jnp.sum(jnp.where(diag, r, 0.0), axis=0, keepdims=True)

    _token_pipeline(idx_hbm, scratch, tbl_ref, compute)
    gi = lax.broadcasted_iota(jnp.int32, (width, PEER_SEL), 0)
    gj = lax.broadcasted_iota(jnp.int32, (width, PEER_SEL), 1)
    group_sum = ((gi >> PEER_CHUNK_SHIFT) == gj).astype(BF16)
    r = _dot(jnp.concatenate(_split3(hb_ref[...]), axis=0), group_sum)
    h = r[0:tb] + r[tb:2 * tb] + r[2 * tb:3 * tb]
    a_ref[...] = 0.5 * h * (1.0 + lax.erf(h * (2.0 ** -0.5))) * gate_ref[...]


def _peer_hidden(idx, x, gate, tbl):
    t = idx.shape[0]
    tb = 2 * PEER_SUB_TOKENS
    return pl.pallas_call(
        _peer_u_kernel,
        grid=(t // tb,),
        in_specs=[pl.BlockSpec(memory_space=pl.ANY),
                  pl.BlockSpec((tb, D_MODEL), lambda i: (i, 0)),
                  pl.BlockSpec((tb, PEER_SEL), lambda i: (i, 0)),
                  pl.BlockSpec(memory_space=pltpu.VMEM)],
        out_specs=pl.BlockSpec((tb, PEER_SEL), lambda i: (i, 0)),
        out_shape=jax.ShapeDtypeStruct((t, PEER_SEL), F32),
        scratch_shapes=[pltpu.VMEM((tb, PEER_SEL * PEER_CHUNKS), F32),
                        pltpu.VMEM((tb * PEER_CHUNKS, LANES), F32)] + _peer_scratch(),
        compiler_params=_params(1),
        name="peer_hidden",
    )(idx.reshape(t // PEER_SUB_TOKENS, PEER_SUB_TOKENS, PEER_SEL), x, gate, tbl)


def _peer_v_kernel(idx_hbm, a_ref, tbl_ref, f8_ref, ah_ref, al_ref, *scratch):
    tb = a_ref.shape[0]
    width = PEER_SEL * PEER_CHUNKS
    ei = lax.broadcasted_iota(jnp.int32, (PEER_SEL, width), 0)
    ej = lax.broadcasted_iota(jnp.int32, (PEER_SEL, width), 1)
    expand = (ei == (ej >> PEER_CHUNK_SHIFT)).astype(BF16)
    a = a_ref[...]
    ah = a.astype(BF16)
    al = (a - ah.astype(F32)).astype(BF16)
    ahl = _dot(jnp.concatenate([ah, al], axis=0), expand)
    ah_ref[...] = ahl[0:tb]
    al_ref[...] = ahl[tb:2 * tb]
    lane = lax.broadcasted_iota(jnp.int32, (PEER_CHUNKS, width), 1)
    row = lax.broadcasted_iota(jnp.int32, (PEER_CHUNKS, width), 0)
    diag = (lane & (PEER_CHUNKS - 1)) == row

    def compute(t, w_ref):
        wb = pltpu.bitcast(w_ref[...], BF16)
        a8h = jnp.where(diag, ah_ref[t:t + 1, :], 0.0).astype(BF16)
        a8l = jnp.where(diag, al_ref[t:t + 1, :], 0.0).astype(BF16)
        r = _dot(jnp.concatenate([a8h, a8l], axis=0), wb)
        f8_ref[t * PEER_CHUNKS:(t + 1) * PEER_CHUNKS, :] = r[0:PEER_CHUNKS] + r[PEER_CHUNKS:2 * PEER_CHUNKS]

    _token_pipeline(idx_hbm, scratch, tbl_ref, compute)


def _peer_output(idx, a, tbl):
    t = idx.shape[0]
    tb = 2 * PEER_SUB_TOKENS
    return pl.pallas_call(
        _peer_v_kernel,
        grid=(t // tb,),
        in_specs=[pl.BlockSpec(memory_space=pl.ANY),
                  pl.BlockSpec((tb, PEER_SEL), lambda i: (i, 0)),
                  pl.BlockSpec(memory_space=pltpu.VMEM)],
        out_specs=pl.BlockSpec((tb * PEER_CHUNKS, LANES), lambda i: (i, 0)),
        out_shape=jax.ShapeDtypeStruct((t * PEER_CHUNKS, LANES), F32),
        scratch_shapes=[pltpu.VMEM((tb, PEER_SEL * PEER_CHUNKS), F32),
                        pltpu.VMEM((tb, PEER_SEL * PEER_CHUNKS), F32)] + _peer_scratch(),
        compiler_params=_params(1),
        name="peer_output",
    )(idx.reshape(t // PEER_SUB_TOKENS, PEER_SUB_TOKENS, PEER_SEL), a, tbl)


def _add_norm_kernel(x_ref, f8_ref, g_ref, b_ref, o_ref):
    tm = x_ref.shape[0]
    f = jnp.concatenate([f8_ref[pl.ds(c, tm, stride=PEER_CHUNKS), :] for c in range(PEER_CHUNKS)], axis=1)
    o_ref[...] = _layer_norm(DN_ALPHA * x_ref[...] + f, g_ref[...], b_ref[...])


def _add_norm(x, f8, g, b, *, tm=512):
    t = x.shape[0]
    row = lambda i: (i, 0)
    const = lambda i: (0, 0)
    return pl.pallas_call(
        _add_norm_kernel,
        grid=(t // tm,),
        in_specs=[pl.BlockSpec((tm, D_MODEL), row), pl.BlockSpec((tm * PEER_CHUNKS, LANES), row),
                  pl.BlockSpec((1, D_MODEL), const), pl.BlockSpec((1, D_MODEL), const)],
        out_specs=pl.BlockSpec((tm, D_MODEL), row),
        out_shape=jax.ShapeDtypeStruct((t, D_MODEL), F32),
        compiler_params=_params(1),
        name="add_norm",
    )(x, f8, g, b)


def _peer(x, w_q, sub_keys, u, v, g, b):
    idx, gate = _peer_route(x, w_q.astype(BF16), sub_keys.astype(BF16))
    a = _peer_hidden(idx, x, gate, _pack_table(u))
    f8 = _peer_output(idx, a, _pack_table(v))
    return _add_norm(x, f8, g, b)


def _pad_heads(w, heads, dim, dim_pad, axis=-1):
    axis = axis % w.ndim
    shape = w.shape[:axis] + (heads, dim) + w.shape[axis + 1:]
    pad = [(0, 0)] * (w.ndim + 1)
    pad[axis + 1] = (0, dim_pad - dim)
    out = jnp.pad(w.reshape(shape), pad)
    return out.reshape(w.shape[:axis] + (heads * dim_pad,) + w.shape[axis + 1:])


def kernel(x, mem, a_w_in, a_w_gate2, a_b_gate, a_norm_g, b_w_in, shared_w_kv, w_mem_kv, w_out, ln_mix_g,
           ln_mix_b, ln_ffn_g, ln_ffn_b, peer_w_q, peer_sub_keys, peer_u, peer_v):
    batch, seq, d = x.shape
    t = batch * seq
    xt = x.reshape(t, d)
    mem2 = mem.reshape(batch * MEM_LEN, d)
    row = lambda a: a.reshape(1, -1)
    qk_w = GLA_HEADS * GLA_DK
    v_w = GLA_HEADS * GLA_DV
    kd = None
    vd = None
    for l in range(DEPTH):
        (mkv,) = _matmul(mem2, w_mem_kv[l].astype(BF16), [2 * MEM_WIDTH], [F32])
        w_mem = w_out[l, v_w:].astype(BF16)
        if l < N_A_LAYERS:
            w = a_w_in[l]
            q0, k0, v0, r0 = 0, qk_w, 2 * qk_w, 2 * qk_w + v_w
            g0 = r0 + v_w
            m0 = g0 + GLA_GATE_RANK
            w_in = jnp.concatenate([
                _pad_heads(w[:, q0:k0], GLA_HEADS, GLA_DK, GLA_DK_PAD),
                _pad_heads(w[:, k0:v0], GLA_HEADS, GLA_DK, GLA_DK_PAD),
                _pad_heads(w[:, v0:r0], GLA_HEADS, GLA_DV, GLA_DV_PAD),
                _pad_heads(w[:, r0:g0], GLA_HEADS, GLA_DV, GLA_DV_PAD),
                w[:, m0:m0 + MEM_WIDTH],
                jnp.pad(w[:, g0:m0], ((0, 0), (0, LANES - GLA_GATE_RANK)))], axis=1).astype(BF16)
            qkw, vw = GLA_HEADS * GLA_DK_PAD, GLA_HEADS * GLA_DV_PAD
            q, k, v, r, qm, gin = _matmul(xt, w_in, [qkw, qkw, vw, vw, MEM_WIDTH, LANES], [F32] * 6)
            wg = jnp.pad(_pad_heads(a_w_gate2[l], GLA_HEADS, GLA_DK, GLA_DK_PAD),
                         ((0, LANES - GLA_GATE_RANK), (0, 0))).astype(BF16)
            bg = row(_pad_heads(a_b_gate[l], GLA_HEADS, GLA_DK, GLA_DK_PAD))
            ng = row(_pad_heads(a_norm_g[l], GLA_HEADS, GLA_DV, GLA_DV_PAD))
            mix = _gla(q, k, v, r, gin, wg, bg, ng, batch=batch, seq=seq)
            w_mix = _pad_heads(w_out[l, :v_w], GLA_HEADS, GLA_DV, GLA_DV_PAD, axis=0).astype(BF16)
            mixes, n_groups = [mix], 0
        else:
            w_in = b_w_in[l - N_A_LAYERS].astype(BF16)
            n_g = len(DIL_PAIRS)
            outs = _matmul(xt, w_in, [DIL_KV_WIDTH] * n_g + [MEM_WIDTH], [F32] * (n_g + 1))
            qm = outs[n_g]
            res = [_dilated_group(outs[g], kd, vd, batch=batch, seq=seq, window=wd, dilation=dl,
                                  heads_per_step=_dilated_heads_per_step(dl))
                   for g, (wd, dl) in enumerate(DIL_PAIRS)]
            mixes = [o for o, _ in res] + [s for _, s in res]
            n_groups = n_g
            w_mix = w_out[l, :DIL_KV_WIDTH].astype(BF16)
        xt = _post_mixer(xt, qm, mkv, w_mix, w_mem, row(ln_mix_g[l]), row(ln_mix_b[l]), mixes,
                         seq=seq, n_groups=n_groups)
        xt = _peer(xt, peer_w_q[l], peer_sub_keys[l], peer_u[l], peer_v[l], row(ln_ffn_g[l]), row(ln_ffn_b[l]))
        if l == N_A_LAYERS - 1:
            kd, vd = _matmul(xt, shared_w_kv.astype(BF16), [DIL_KV_WIDTH] * 2, [F32] * 2)
    return xt.reshape(batch, seq, d)
```

```python
import functools

import jax
import jax.numpy as jnp
from jax import lax
from jax.experimental import pallas as pl
from jax.experimental.pallas import tpu as pltpu

D_MODEL = 1024
DEPTH = 2
N_A_LAYERS = 1
DN_ALPHA = (2.0 * DEPTH) ** 0.25
LN_EPS = 1e-5
HEAD_NORM_EPS = 1e-6

MEM_LEN = 256
MEM_HEADS = 4
MEM_HEAD_DIM = 64
MEM_HEAD_SHIFT = 6
MEM_WIDTH = 256

GLA_HEADS = 4
GLA_DK = 96
GLA_DV = 192
GLA_DK_PAD = 128
GLA_DV_PAD = 256
GLA_GATE_RANK = 16
GLA_TAU = 16.0
GLA_CHUNK = 64
GLA_CHUNK_SHIFT = 6

DIL_PAIRS = ((128, 1), (512, 4), (2048, 16))
DIL_SLOTS = 6
DIL_HEAD_DIM = 128
DIL_BLOCK = 128
DIL_KV_WIDTH = DIL_SLOTS * DIL_HEAD_DIM

PEER_N_KEYS = 128
PEER_HEADS = 8
PEER_TOPK = 16
PEER_HALF = 128
PEER_SEL = PEER_HEADS * PEER_TOPK
PEER_ROW_WORDS = 4
PEER_CHUNKS = D_MODEL // 128
PEER_CHUNK_SHIFT = 3
PEER_HIDDEN_BLOCK = 32
PEER_OUTPUT_BLOCK = 32
PEER_SUB_TOKENS = 32
PEER_TOPK_SHIFT = 4

LANES = 128
VMEM_LIMIT_BYTES = 56 * 1024 * 1024

BF16 = jnp.bfloat16
F32 = jnp.float32


def _params(n_grid_dims):
    return pltpu.CompilerParams(dimension_semantics=("arbitrary",) * n_grid_dims,
                                vmem_limit_bytes=VMEM_LIMIT_BYTES)


def _dot(a, b):
    return jnp.dot(a, b, preferred_element_type=F32)


def _dot_nt(a, b):
    return lax.dot_general(a, b, (((1,), (1,)), ((), ())), preferred_element_type=F32)


def _dot_tn(a, b):
    return lax.dot_general(a, b, (((0,), (0,)), ((), ())), preferred_element_type=F32)


def _split3(v):
    hi = v.astype(BF16)
    r1 = v - hi.astype(F32)
    mid = r1.astype(BF16)
    lo = (r1 - mid.astype(F32)).astype(BF16)
    return hi, mid, lo


def _layer_norm(z, g, b):
    mu = jnp.mean(z, axis=-1, keepdims=True)
    zc = z - mu
    var = jnp.mean(zc * zc, axis=-1, keepdims=True)
    return zc * lax.rsqrt(var + LN_EPS) * g + b


def _mm_kernel(x_ref, w_ref, *o_refs, splits):
    y = _dot(x_ref[...].astype(BF16), w_ref[...])
    off = 0
    for o_ref, width in zip(o_refs, splits):
        o_ref[...] = y[:, off:off + width].astype(o_ref.dtype)
        off += width


def _matmul(x, w, splits, dtypes, *, tm=256):
    t, k = x.shape
    n = w.shape[1]
    assert sum(splits) == n and t % tm == 0
    return pl.pallas_call(
        functools.partial(_mm_kernel, splits=tuple(splits)),
        grid=(t // tm,),
        in_specs=[pl.BlockSpec((tm, k), lambda i: (i, 0)),
                  pl.BlockSpec((k, n), lambda i: (0, 0))],
        out_specs=[pl.BlockSpec((tm, s), lambda i: (i, 0)) for s in splits],
        out_shape=[jax.ShapeDtypeStruct((t, s), d) for s, d in zip(splits, dtypes)],
        compiler_params=_params(1),
        name="matmul",
    )(x, w)


def _gla_kernel(q_ref, k_ref, v_ref, r_ref, g_ref, wg_ref, bg_ref, ng_ref, o_ref, state_ref, *, rows):
    nchunk = rows // GLA_CHUNK

    @pl.when(pl.program_id(1) == 0)
    def _():
        state_ref[...] = jnp.zeros_like(state_ref)

    g_pre = _dot(g_ref[...].astype(BF16), wg_ref[...]) + bg_ref[...]
    log_a = (jnp.minimum(g_pre, 0.0) - jnp.log1p(jnp.exp(-jnp.abs(g_pre)))) / GLA_TAU

    ri = lax.broadcasted_iota(jnp.int32, (rows, rows), 0)
    ci = lax.broadcasted_iota(jnp.int32, (rows, rows), 1)
    same_chunk_causal = (ci <= ri) & ((ri >> GLA_CHUNK_SHIFT) == (ci >> GLA_CHUNK_SHIFT))
    tril = same_chunk_causal.astype(BF16)
    b = jnp.zeros_like(log_a)
    for part in _split3(log_a):
        b = b + _dot(tril, part)
    b_last = jnp.concatenate(
        [jnp.broadcast_to(b[(c + 1) * GLA_CHUNK - 1:(c + 1) * GLA_CHUNK, :], (GLA_CHUNK, b.shape[1]))
         for c in range(nchunk)], axis=0)

    q = q_ref[...] * (GLA_DK ** -0.5)
    k = k_ref[...]
    q_t = (q * jnp.exp(b)).astype(BF16)
    k_t = (k * jnp.exp(-b)).astype(BF16)
    k_end = (k * jnp.exp(b_last - b)).astype(BF16)
    decay = jnp.exp(b_last)
    v = v_ref[...].astype(BF16)
    r = r_ref[...]
    ng = ng_ref[...]

    for h in range(GLA_HEADS):
        ks = slice(h * GLA_DK_PAD, (h + 1) * GLA_DK_PAD)
        vs = slice(h * GLA_DV_PAD, (h + 1) * GLA_DV_PAD)
        attn = jnp.where(same_chunk_causal, _dot_nt(q_t[:, ks], k_t[:, ks]), 0.0).astype(BF16)
        o_intra = _dot(attn, v[:, vs])
        st = state_ref[h]
        o_inter = []
        for c in range(nchunk):
            rs = slice(c * GLA_CHUNK, (c + 1) * GLA_CHUNK)
            o_inter.append(_dot_nt(q_t[rs, ks], st.astype(BF16)))
            st = st * decay[c * GLA_CHUNK:c * GLA_CHUNK + 1, ks] + _dot_tn(v[rs, vs], k_end[rs, ks])
        state_ref[h] = st
        o = o_intra + jnp.concatenate(o_inter, axis=0)
        ms = jnp.sum(o * o, axis=-1, keepdims=True) / GLA_DV
        o = o * lax.rsqrt(ms + HEAD_NORM_EPS) * ng[:, vs]
        rh = r[:, vs]
        o_ref[:, vs] = (rh * jax.nn.sigmoid(rh) * o).astype(o_ref.dtype)


def _gla(q, k, v, r, g, wg, bg, ng, *, batch, seq, rows=256):
    nblk = seq // rows
    row_map = lambda b, i: (b * nblk + i, 0)
    const = lambda b, i: (0, 0)
    qk_w, v_w = GLA_HEADS * GLA_DK_PAD, GLA_HEADS * GLA_DV_PAD
    return pl.pallas_call(
        functools.partial(_gla_kernel, rows=rows),
        grid=(batch, nblk),
        in_specs=[pl.BlockSpec((rows, qk_w), row_map), pl.BlockSpec((rows, qk_w), row_map),
                  pl.BlockSpec((rows, v_w), row_map), pl.BlockSpec((rows, v_w), row_map),
                  pl.BlockSpec((rows, LANES), row_map),
                  pl.BlockSpec((LANES, qk_w), const), pl.BlockSpec((1, qk_w), const),
                  pl.BlockSpec((1, v_w), const)],
        out_specs=pl.BlockSpec((rows, v_w), row_map),
        out_shape=jax.ShapeDtypeStruct((batch * seq, v_w), BF16),
        scratch_shapes=[pltpu.VMEM((GLA_HEADS, GLA_DV_PAD, GLA_DK_PAD), F32)],
        compiler_params=_params(2),
        name="gla",
    )(q, k, v, r, g, wg, bg, ng)


def _dilated_kernel(q_ref, kp_ref, kc_ref, vp_ref, vc_ref, o_ref, lse_ref, *, span, dilation, heads):
    first_key = jnp.where(pl.program_id(1) == 0, DIL_BLOCK, 0)
    qi = lax.broadcasted_iota(jnp.int32, (DIL_BLOCK, 2 * DIL_BLOCK), 0)
    kj = lax.broadcasted_iota(jnp.int32, (DIL_BLOCK, 2 * DIL_BLOCK), 1)
    rel = DIL_BLOCK + qi - kj
    mask = (rel >= 0) & (rel <= span) & (kj >= first_key)
    scale = DIL_HEAD_DIM ** -0.5
    for r in range(dilation):
        rows = pl.ds(r, DIL_BLOCK, stride=dilation) if dilation > 1 else slice(None)
        for h in range(heads):
            hs = slice(h * DIL_HEAD_DIM, (h + 1) * DIL_HEAD_DIM)
            q = q_ref[rows, hs].astype(BF16)
            kcat = jnp.concatenate([kp_ref[rows, hs], kc_ref[rows, hs]], axis=0).astype(BF16)
            vcat = jnp.concatenate([vp_ref[rows, hs], vc_ref[rows, hs]], axis=0).astype(BF16)
            s = jnp.where(mask, _dot_nt(q, kcat) * scale, -jnp.inf)
            m = jnp.max(s, axis=-1, keepdims=True)
            p = jnp.exp(s - m)
            l = jnp.sum(p, axis=-1, keepdims=True)
            o_ref[rows, hs] = _dot(p.astype(BF16), vcat) / l
            lse_ref[rows, hs] = jnp.broadcast_to(m + jnp.log(l), (DIL_BLOCK, DIL_HEAD_DIM))


def _dilated_heads_per_step(dilation):
    return DIL_SLOTS if dilation == 1 else 1


def _dilated_group(q, k, v, *, batch, seq, window, dilation, heads_per_step):
    rows = dilation * DIL_BLOCK
    nb = seq // rows
    cur = lambda b, n, hg: (b * nb + n, hg)
    prev = lambda b, n, hg: (b * nb + jnp.maximum(n - 1, 0), hg)
    blk = (rows, heads_per_step * DIL_HEAD_DIM)
    return pl.pallas_call(
        functools.partial(_dilated_kernel, span=window // dilation, dilation=dilation, heads=heads_per_step),
        grid=(batch, nb, DIL_SLOTS // heads_per_step),
        in_specs=[pl.BlockSpec(blk, cur), pl.BlockSpec(blk, prev), pl.BlockSpec(blk, cur),
                  pl.BlockSpec(blk, prev), pl.BlockSpec(blk, cur)],
        out_specs=[pl.BlockSpec(blk, cur), pl.BlockSpec(blk, cur)],
        out_shape=[jax.ShapeDtypeStruct((batch * seq, DIL_KV_WIDTH), F32)] * 2,
        compiler_params=_params(3),
        name="dilated_attention",
    )(q, k, k, v, v)


def _post_kernel(*refs, n_groups):
    x_ref, qm_ref, km_ref, vm_ref, wmix_ref, wmem_ref, g_ref, b_ref = refs[:8]
    mix_refs = refs[8:-1]
    o_ref = refs[-1]
    if n_groups == 0:
        mix = mix_refs[0][...]
    else:
        lses = [mix_refs[n_groups + g][...] for g in range(n_groups)]
        mx = functools.reduce(jnp.maximum, lses)
        ws = [jnp.exp(l - mx) for l in lses]
        den = functools.reduce(lambda a, c: a + c, ws)
        mix = functools.reduce(lambda a, c: a + c, [w * mix_refs[g][...] for g, w in enumerate(ws)]) / den
    qm = qm_ref[...]
    km = km_ref[...].astype(BF16)
    vm = vm_ref[...].astype(BF16)
    lane = lax.broadcasted_iota(jnp.int32, qm.shape, 1)
    mo = jnp.zeros(qm.shape, F32)
    for h in range(MEM_HEADS):
        head = (lane >> MEM_HEAD_SHIFT) == h
        s = _dot_nt(jnp.where(head, qm, 0.0).astype(BF16), km) * (MEM_HEAD_DIM ** -0.5)
        m = jnp.max(s, axis=-1, keepdims=True)
        p = jnp.exp(s - m)
        p = p / jnp.sum(p, axis=-1, keepdims=True)
        mo = mo + jnp.where(head, _dot(p.astype(BF16), vm), 0.0)
    y = _dot(mix.astype(BF16), wmix_ref[...]) + _dot(mo.astype(BF16), wmem_ref[...])
    o_ref[...] = _layer_norm(DN_ALPHA * x_ref[...] + y, g_ref[...], b_ref[...])


def _post_mixer(x, qm, mkv, w_mix, w_mem, g, b, mixes, *, seq, n_groups, tm=256):
    t = x.shape[0]
    per_batch = seq // tm
    row = lambda i: (i, 0)
    const = lambda i: (0, 0)
    mix_w = w_mix.shape[0]
    in_specs = [pl.BlockSpec((tm, D_MODEL), row), pl.BlockSpec((tm, MEM_WIDTH), row),
                pl.BlockSpec((MEM_LEN, MEM_WIDTH), lambda i: (i // per_batch, 0)),
                pl.BlockSpec((MEM_LEN, MEM_WIDTH), lambda i: (i // per_batch, 1)),
                pl.BlockSpec((mix_w, D_MODEL), const), pl.BlockSpec((MEM_WIDTH, D_MODEL), const),
                pl.BlockSpec((1, D_MODEL), const), pl.BlockSpec((1, D_MODEL), const)]
    in_specs += [pl.BlockSpec((tm, mix_w), row) for _ in mixes]
    return pl.pallas_call(
        functools.partial(_post_kernel, n_groups=n_groups),
        grid=(t // tm,),
        in_specs=in_specs,
        out_specs=pl.BlockSpec((tm, D_MODEL), row),
        out_shape=jax.ShapeDtypeStruct((t, D_MODEL), F32),
        compiler_params=_params(1),
        name="post_mixer",
    )(x, qm, mkv, mkv, w_mix, w_mem, g, b, *mixes)


def _top_rows(s, n_rows, val_ref, idx_ref):
    iota = lax.broadcasted_iota(jnp.int32, s.shape, 0).astype(F32)
    for r in range(PEER_TOPK):
        m = jnp.max(s, axis=0, keepdims=True)
        am = jnp.min(jnp.where(s == m, iota, float(n_rows)), axis=0, keepdims=True)
        val_ref[r:r + 1, :] = m
        idx_ref[r:r + 1, :] = am
        s = jnp.where(iota == am, -jnp.inf, s)


PAIR_COUNTS = tuple(PEER_TOPK // (a + 1) for a in range(PEER_TOPK))
PAIR_OFFSETS = tuple(sum(PAIR_COUNTS[:a]) for a in range(PEER_TOPK))
PAIR_ROWS = -(-sum(PAIR_COUNTS) // 8) * 8


def _route_kernel(x_ref, wq_ref, keys_ref, idx_ref, gate_ref, q_ref, tv_ref, ti_ref, bv_ref, bi_ref,
                  cand_ref, e_ref, gt_ref):
    tm = x_ref.shape[0]
    q_ref[...] = _dot(x_ref[...].astype(BF16), wq_ref[...]).astype(BF16)
    n_pairs = sum(PAIR_COUNTS)
    cand_ref[n_pairs:, :] = jnp.full((PAIR_ROWS - n_pairs, tm), -jnp.inf, F32)
    for h in range(PEER_HEADS):
        for p in range(2):
            c0 = (h * 2 + p) * PEER_HALF
            s = _dot_nt(keys_ref[p], q_ref[:, c0:c0 + PEER_HALF])
            _top_rows(s, PEER_N_KEYS, tv_ref.at[p], ti_ref.at[p])
        for a in range(PEER_TOPK):
            cand_ref[PAIR_OFFSETS[a]:PAIR_OFFSETS[a] + PAIR_COUNTS[a], :] = (
                tv_ref[0, a:a + 1, :] + tv_ref[1, 0:PAIR_COUNTS[a], :])
        _top_rows(cand_ref[...], PAIR_ROWS, bv_ref, bi_ref)
        best_s = bv_ref[...]
        best_r = bi_ref[...].astype(jnp.int32)
        ia = jnp.zeros(best_r.shape, F32)
        ib = jnp.zeros(best_r.shape, F32)
        row0 = jnp.zeros(best_r.shape, jnp.int32)
        for a in range(PEER_TOPK):
            at_least = best_r >= PAIR_OFFSETS[a]
            ia = jnp.where(at_least, ti_ref[0, a:a + 1, :], ia)
            row0 = jnp.where(at_least, PAIR_OFFSETS[a], row0)
        jb = best_r - row0
        for b in range(PEER_TOPK):
            ib = jnp.where(jb == b, ti_ref[1, b:b + 1, :], ib)
        pe = jnp.exp(best_s - best_s[0:1, :])
        rs = slice(h * PEER_TOPK, (h + 1) * PEER_TOPK)
        e_ref[rs, :] = (ia.astype(jnp.int32) * PEER_N_KEYS + ib.astype(jnp.int32)) * PEER_ROW_WORDS
        gt_ref[rs, :] = pe / jnp.sum(pe, axis=0, keepdims=True)
    idx_ref[...] = pltpu.bitcast(pltpu.bitcast(e_ref[...], F32).T, jnp.int32)
    gate_ref[...] = gt_ref[...].T


def _peer_route(x, wq, keys, *, tm=256):
    t = x.shape[0]
    nq = wq.shape[1]
    return pl.pallas_call(
        _route_kernel,
        grid=(t // tm,),
        in_specs=[pl.BlockSpec((tm, D_MODEL), lambda i: (i, 0)),
                  pl.BlockSpec((D_MODEL, nq), lambda i: (0, 0)),
                  pl.BlockSpec((2, PEER_N_KEYS, PEER_HALF), lambda i: (0, 0, 0))],
        out_specs=[pl.BlockSpec((tm, PEER_SEL), lambda i: (i, 0))] * 2,
        out_shape=[jax.ShapeDtypeStruct((t, PEER_SEL), jnp.int32),
                   jax.ShapeDtypeStruct((t, PEER_SEL), F32)],
        scratch_shapes=[pltpu.VMEM((tm, nq), BF16),
                        pltpu.VMEM((2, PEER_TOPK, tm), F32), pltpu.VMEM((2, PEER_TOPK, tm), F32),
                        pltpu.VMEM((PEER_TOPK, tm), F32), pltpu.VMEM((PEER_TOPK, tm), F32),
                        pltpu.VMEM((PAIR_ROWS, tm), F32),
                        pltpu.VMEM((PEER_SEL, tm), jnp.int32), pltpu.VMEM((PEER_SEL, tm), F32)],
        compiler_params=_params(1),
        name="peer_route",
    )(x, wq, keys)


def _pack_kernel(w_ref, o_ref):
    te = w_ref.shape[0]
    for s in range(PEER_ROW_WORDS):
        lo = w_ref[:, 2 * s * LANES:(2 * s + 1) * LANES].astype(BF16).astype(F32)
        hi = w_ref[:, (2 * s + 1) * LANES:(2 * s + 2) * LANES].astype(BF16).astype(F32)
        word = (pltpu.bitcast(lo, jnp.uint32) >> 16) | (pltpu.bitcast(hi, jnp.uint32) & jnp.uint32(0xFFFF0000))
        o_ref[pl.ds(s, te, stride=PEER_ROW_WORDS), :] = word


def _pack_table(w, *, te=512):
    e = w.shape[0]
    return pl.pallas_call(
        _pack_kernel,
        grid=(e // te,),
        in_specs=[pl.BlockSpec((te, D_MODEL), lambda i: (i, 0))],
        out_specs=pl.BlockSpec((te * PEER_ROW_WORDS, LANES), lambda i: (i, 0)),
        out_shape=jax.ShapeDtypeStruct((e * PEER_ROW_WORDS, LANES), jnp.uint32),
        compiler_params=_params(1),
        name="pack_table",
    )(w)


def _expert_block(idx_ref, t, blk, size, tbl_ref):
    rows = []
    for j in range(blk * size, (blk + 1) * size):
        i = pl.multiple_of(idx_ref[t, j], PEER_ROW_WORDS)
        rows.append(tbl_ref[pl.ds(i, PEER_ROW_WORDS), :])
    return pltpu.bitcast(jnp.concatenate(rows, axis=0), BF16)


def _peer_scratch():
    return [pltpu.SMEM((PEER_SUB_TOKENS, PEER_SEL), jnp.int32)] * 2 + [pltpu.SemaphoreType.DMA(())] * 2


def _token_loop(idx_hbm, scratch, compute):
    tbs = PEER_SUB_TOKENS
    idx_a, idx_b, sem_a, sem_b = scratch
    step = pl.program_id(0)
    nxt = jnp.minimum(step + 1, pl.num_programs(0) - 1)

    def copy_a(k):
        return pltpu.make_async_copy(idx_hbm.at[2 * k], idx_a, sem_a)

    def copy_b(k):
        return pltpu.make_async_copy(idx_hbm.at[2 * k + 1], idx_b, sem_b)

    @pl.when(step == 0)
    def _():
        copy_a(0).start()
        copy_a(0).wait()

    copy_b(step).start()
    for g in range(tbs):
        compute(g, idx_a, g)
    copy_a(nxt).start()
    copy_b(step).wait()
    for g in range(tbs):
        compute(tbs + g, idx_b, g)
    copy_a(nxt).wait()


def _peer_u_kernel(idx_hbm, x_ref, gate_ref, tbl_ref, a_ref, hb_ref, x8_ref, *scratch):
    tb = x_ref.shape[0]
    width = PEER_SEL * PEER_CHUNKS
    for c in range(PEER_CHUNKS):
        x8_ref[pl.ds(c, tb, stride=PEER_CHUNKS), :] = x_ref[:, c * LANES:(c + 1) * LANES]
    lane = lax.broadcasted_iota(jnp.int32, (2 * PEER_CHUNKS, width), 1)
    row = lax.broadcasted_iota(jnp.int32, (2 * PEER_CHUNKS, width), 0)
    diag = (lane & (PEER_CHUNKS - 1)) == (row & (PEER_CHUNKS - 1))

    def compute(g, idx_ref, t):
        x8 = x8_ref[g * PEER_CHUNKS:(g + 1) * PEER_CHUNKS, :]
        xh = x8.astype(BF16)
        xl = (x8 - xh.astype(F32)).astype(BF16)
        x16 = jnp.concatenate([xh, xl], axis=0)
        r = jnp.concatenate([_dot_nt(x16, _expert_block(idx_ref, t, blk, PEER_HIDDEN_BLOCK, tbl_ref))
                             for blk in range(PEER_SEL // PEER_HIDDEN_BLOCK)], axis=1)
        hb_ref[g:g + 1, :] = jnp.sum(jnp.where(diag, r, 0.0), axis=0, keepdims=True)

    _token_loop(idx_hbm, scratch, compute)
    gi = lax.broadcasted_iota(jnp.int32, (width, PEER_SEL), 0)
    gj = lax.broadcasted_iota(jnp.int32, (width, PEER_SEL), 1)
    group_sum = ((gi >> PEER_CHUNK_SHIFT) == gj).astype(BF16)
    r = _dot(jnp.concatenate(_split3(hb_ref[...]), axis=0), group_sum)
    h = r[0:tb] + r[tb:2 * tb] + r[2 * tb:3 * tb]
    a_ref[...] = 0.5 * h * (1.0 + lax.erf(h * (2.0 ** -0.5))) * gate_ref[...]


def _peer_hidden(idx, x, gate, tbl):
    t = idx.shape[0]
    tb = 2 * PEER_SUB_TOKENS
    return pl.pallas_call(
        _peer_u_kernel,
        grid=(t // tb,),
        in_specs=[pl.BlockSpec(memory_space=pl.ANY),
                  pl.BlockSpec((tb, D_MODEL), lambda i: (i, 0)),
                  pl.BlockSpec((tb, PEER_SEL), lambda i: (i, 0)),
                  pl.BlockSpec(memory_space=pltpu.VMEM)],
        out_specs=pl.BlockSpec((tb, PEER_SEL), lambda i: (i, 0)),
        out_shape=jax.ShapeDtypeStruct((t, PEER_SEL), F32),
        scratch_shapes=[pltpu.VMEM((tb, PEER_SEL * PEER_CHUNKS), F32),
                        pltpu.VMEM((tb * PEER_CHUNKS, LANES), F32)] + _peer_scratch(),
        compiler_params=_params(1),
        name="peer_hidden",
    )(idx.reshape(t // PEER_SUB_TOKENS, PEER_SUB_TOKENS, PEER_SEL), x, gate, tbl)


def _peer_v_kernel(idx_hbm, a_ref, tbl_ref, f8_ref, ah_ref, al_ref, *scratch):
    tb = a_ref.shape[0]
    width = PEER_SEL * PEER_CHUNKS
    ei = lax.broadcasted_iota(jnp.int32, (PEER_SEL, width), 0)
    ej = lax.broadcasted_iota(jnp.int32, (PEER_SEL, width), 1)
    expand = (ei == (ej >> PEER_CHUNK_SHIFT)).astype(BF16)
    a = a_ref[...]
    ah = a.astype(BF16)
    al = (a - ah.astype(F32)).astype(BF16)
    ahl = _dot(jnp.concatenate([ah, al], axis=0), expand)
    ah_ref[...] = ahl[0:tb]
    al_ref[...] = ahl[tb:2 * tb]
    lane = lax.broadcasted_iota(jnp.int32, (PEER_CHUNKS, width), 1)
    row = lax.broadcasted_iota(jnp.int32, (PEER_CHUNKS, width), 0)
    diag = (lane & (PEER_CHUNKS - 1)) == row

    def compute(g, idx_ref, t):
        a8h = jnp.where(diag, ah_ref[g:g + 1, :], 0.0).astype(BF16)
        a8l = jnp.where(diag, al_ref[g:g + 1, :], 0.0).astype(BF16)
        a16 = jnp.concatenate([a8h, a8l], axis=0)
        lanes = PEER_OUTPUT_BLOCK * PEER_CHUNKS
        r = None
        for blk in range(PEER_SEL // PEER_OUTPUT_BLOCK):
            part = _dot(a16[:, blk * lanes:(blk + 1) * lanes],
                        _expert_block(idx_ref, t, blk, PEER_OUTPUT_BLOCK, tbl_ref))
            r = part if r is None else r + part
        f8_ref[g * PEER_CHUNKS:(g + 1) * PEER_CHUNKS, :] = r[0:PEER_CHUNKS] + r[PEER_CHUNKS:2 * PEER_CHUNKS]

    _token_loop(idx_hbm, scratch, compute)


def _peer_output(idx, a, tbl):
    t = idx.shape[0]
    tb = 2 * PEER_SUB_TOKENS
    return pl.pallas_call(
        _peer_v_kernel,
        grid=(t // tb,),
        in_specs=[pl.BlockSpec(memory_space=pl.ANY),
                  pl.BlockSpec((tb, PEER_SEL), lambda i: (i, 0)),
                  pl.BlockSpec(memory_space=pltpu.VMEM)],
        out_specs=pl.BlockSpec((tb * PEER_CHUNKS, LANES), lambda i: (i, 0)),
        out_shape=jax.ShapeDtypeStruct((t * PEER_CHUNKS, LANES), F32),
        scratch_shapes=[pltpu.VMEM((tb, PEER_SEL * PEER_CHUNKS), F32),
                        pltpu.VMEM((tb, PEER_SEL * PEER_CHUNKS), F32)] + _peer_scratch(),
        compiler_params=_params(1),
        name="peer_output",
    )(idx.reshape(t // PEER_SUB_TOKENS, PEER_SUB_TOKENS, PEER_SEL), a, tbl)


def _add_norm_kernel(x_ref, f8_ref, g_ref, b_ref, o_ref):
    tm = x_ref.shape[0]
    f = jnp.concatenate([f8_ref[pl.ds(c, tm, stride=PEER_CHUNKS), :] for c in range(PEER_CHUNKS)], axis=1)
    o_ref[...] = _layer_norm(DN_ALPHA * x_ref[...] + f, g_ref[...], b_ref[...])


def _add_norm(x, f8, g, b, *, tm=512):
    t = x.shape[0]
    row = lambda i: (i, 0)
    const = lambda i: (0, 0)
    return pl.pallas_call(
        _add_norm_kernel,
        grid=(t // tm,),
        in_specs=[pl.BlockSpec((tm, D_MODEL), row), pl.BlockSpec((tm * PEER_CHUNKS, LANES), row),
                  pl.BlockSpec((1, D_MODEL), const), pl.BlockSpec((1, D_MODEL), const)],
        out_specs=pl.BlockSpec((tm, D_MODEL), row),
        out_shape=jax.ShapeDtypeStruct((t, D_MODEL), F32),
        compiler_params=_params(1),
        name="add_norm",
    )(x, f8, g, b)


def _peer(x, w_q, sub_keys, u, v, g, b):
    idx, gate = _peer_route(x, w_q.astype(BF16), sub_keys.astype(BF16))
    a = _peer_hidden(idx, x, gate, _pack_table(u))
    f8 = _peer_output(idx, a, _pack_table(v))
    return _add_norm(x, f8, g, b)


def _pad_heads(w, heads, dim, dim_pad, axis=-1):
    axis = axis % w.ndim
    shape = w.shape[:axis] + (heads, dim) + w.shape[axis + 1:]
    pad = [(0, 0)] * (w.ndim + 1)
    pad[axis + 1] = (0, dim_pad - dim)
    out = jnp.pad(w.reshape(shape), pad)
    return out.reshape(w.shape[:axis] + (heads * dim_pad,) + w.shape[axis + 1:])


def kernel(x, mem, a_w_in, a_w_gate2, a_b_gate, a_norm_g, b_w_in, shared_w_kv, w_mem_kv, w_out, ln_mix_g,
           ln_mix_b, ln_ffn_g, ln_ffn_b, peer_w_q, peer_sub_keys, peer_u, peer_v):
    batch, seq, d = x.shape
    t = batch * seq
    xt = x.reshape(t, d)
    mem2 = mem.reshape(batch * MEM_LEN, d)
    row = lambda a: a.reshape(1, -1)
    qk_w = GLA_HEADS * GLA_DK
    v_w = GLA_HEADS * GLA_DV
    kd = None
    vd = None
    for l in range(DEPTH):
        (mkv,) = _matmul(mem2, w_mem_kv[l].astype(BF16), [2 * MEM_WIDTH], [F32])
        w_mem = w_out[l, v_w:].astype(BF16)
        if l < N_A_LAYERS:
            w = a_w_in[l]
            q0, k0, v0, r0 = 0, qk_w, 2 * qk_w, 2 * qk_w + v_w
            g0 = r0 + v_w
            m0 = g0 + GLA_GATE_RANK
            w_in = jnp.concatenate([
                _pad_heads(w[:, q0:k0], GLA_HEADS, GLA_DK, GLA_DK_PAD),
                _pad_heads(w[:, k0:v0], GLA_HEADS, GLA_DK, GLA_DK_PAD),
                _pad_heads(w[:, v0:r0], GLA_HEADS, GLA_DV, GLA_DV_PAD),
                _pad_heads(w[:, r0:g0], GLA_HEADS, GLA_DV, GLA_DV_PAD),
                w[:, m0:m0 + MEM_WIDTH],
                jnp.pad(w[:, g0:m0], ((0, 0), (0, LANES - GLA_GATE_RANK)))], axis=1).astype(BF16)
            qkw, vw = GLA_HEADS * GLA_DK_PAD, GLA_HEADS * GLA_DV_PAD
            q, k, v, r, qm, gin = _matmul(xt, w_in, [qkw, qkw, vw, vw, MEM_WIDTH, LANES], [F32] * 6)
            wg = jnp.pad(_pad_heads(a_w_gate2[l], GLA_HEADS, GLA_DK, GLA_DK_PAD),
                         ((0, LANES - GLA_GATE_RANK), (0, 0))).astype(BF16)
            bg = row(_pad_heads(a_b_gate[l], GLA_HEADS, GLA_DK, GLA_DK_PAD))
            ng = row(_pad_heads(a_norm_g[l], GLA_HEADS, GLA_DV, GLA_DV_PAD))
            mix = _gla(q, k, v, r, gin, wg, bg, ng, batch=batch, seq=seq)
            w_mix = _pad_heads(w_out[l, :v_w], GLA_HEADS, GLA_DV, GLA_DV_PAD, axis=0).astype(BF16)
            mixes, n_groups = [mix], 0
        else:
            w_in = b_w_in[l - N_A_LAYERS].astype(BF16)
            n_g = len(DIL_PAIRS)
            outs = _matmul(xt, w_in, [DIL_KV_WIDTH] * n_g + [MEM_WIDTH], [F32] * (n_g + 1))
            qm = outs[n_g]
            res = [_dilated_group(outs[g], kd, vd, batch=batch, seq=seq, window=wd, dilation=dl,
                                  heads_per_step=_dilated_heads_per_step(dl))
                   for g, (wd, dl) in enumerate(DIL_PAIRS)]
            mixes = [o for o, _ in res] + [s for _, s in res]
            n_groups = n_g
            w_mix = w_out[l, :DIL_KV_WIDTH].astype(BF16)
        xt = _post_mixer(xt, qm, mkv, w_mix, w_mem, row(ln_mix_g[l]), row(ln_mix_b[l]), mixes,
                         seq=seq, n_groups=n_groups)
        xt = _peer(xt, peer_w_q[l], peer_sub_keys[l], peer_u[l], peer_v[l], row(ln_ffn_g[l]), row(ln_ffn_b[l]))
        if l == N_A_LAYERS - 1:
            kd, vd = _matmul(xt, shared_w_kv.astype(BF16), [DIL_KV_WIDTH] * 2, [F32] * 2)
    return xt.reshape(batch, seq, d)
```

```python
import functools

import jax
import jax.numpy as jnp
from jax import lax
from jax.experimental import pallas as pl
from jax.experimental.pallas import tpu as pltpu

D_MODEL = 1024
DEPTH = 2
N_A_LAYERS = 1
DN_ALPHA = (2.0 * DEPTH) ** 0.25
LN_EPS = 1e-5
HEAD_NORM_EPS = 1e-6

MEM_LEN = 256
MEM_HEADS = 4
MEM_HEAD_DIM = 64
MEM_HEAD_SHIFT = 6
MEM_WIDTH = 256

GLA_HEADS = 4
GLA_DK = 96
GLA_DV = 192
GLA_DK_PAD = 128
GLA_DV_PAD = 256
GLA_GATE_RANK = 16
GLA_TAU = 16.0
GLA_CHUNK = 64
GLA_CHUNK_SHIFT = 6

DIL_PAIRS = ((128, 1), (512, 4), (2048, 16))
DIL_SLOTS = 6
DIL_HEAD_DIM = 128
DIL_BLOCK = 128
DIL_KV_WIDTH = DIL_SLOTS * DIL_HEAD_DIM

PEER_N_KEYS = 128
PEER_HEADS = 8
PEER_TOPK = 16
PEER_HALF = 128
PEER_SEL = PEER_HEADS * PEER_TOPK
PEER_ROW_WORDS = 4
PEER_CHUNKS = D_MODEL // 128
PEER_CHUNK_SHIFT = 3
PEER_HIDDEN_BLOCK = 32
PEER_OUTPUT_BLOCK = 32
PEER_SUB_TOKENS = 64
PEER_TOPK_SHIFT = 4

LANES = 128
VMEM_LIMIT_BYTES = 56 * 1024 * 1024

BF16 = jnp.bfloat16
F32 = jnp.float32


def _params(n_grid_dims):
    return pltpu.CompilerParams(dimension_semantics=("arbitrary",) * n_grid_dims,
                                vmem_limit_bytes=VMEM_LIMIT_BYTES)


def _dot(a, b):
    return jnp.dot(a, b, preferred_element_type=F32)


def _dot_nt(a, b):
    return lax.dot_general(a, b, (((1,), (1,)), ((), ())), preferred_element_type=F32)


def _dot_tn(a, b):
    return lax.dot_general(a, b, (((0,), (0,)), ((), ())), preferred_element_type=F32)


def _split3(v):
    hi = v.astype(BF16)
    r1 = v - hi.astype(F32)
    mid = r1.astype(BF16)
    lo = (r1 - mid.astype(F32)).astype(BF16)
    return hi, mid, lo


def _layer_norm(z, g, b):
    mu = jnp.mean(z, axis=-1, keepdims=True)
    zc = z - mu
    var = jnp.mean(zc * zc, axis=-1, keepdims=True)
    return zc * lax.rsqrt(var + LN_EPS) * g + b


def _mm_kernel(x_ref, w_ref, *o_refs, splits):
    y = _dot(x_ref[...].astype(BF16), w_ref[...])
    off = 0
    for o_ref, width in zip(o_refs, splits):
        o_ref[...] = y[:, off:off + width].astype(o_ref.dtype)
        off += width


def _matmul(x, w, splits, dtypes, *, tm=256):
    t, k = x.shape
    n = w.shape[1]
    assert sum(splits) == n and t % tm == 0
    return pl.pallas_call(
        functools.partial(_mm_kernel, splits=tuple(splits)),
        grid=(t // tm,),
        in_specs=[pl.BlockSpec((tm, k), lambda i: (i, 0)),
                  pl.BlockSpec((k, n), lambda i: (0, 0))],
        out_specs=[pl.BlockSpec((tm, s), lambda i: (i, 0)) for s in splits],
        out_shape=[jax.ShapeDtypeStruct((t, s), d) for s, d in zip(splits, dtypes)],
        compiler_params=_params(1),
        name="matmul",
    )(x, w)


def _gla_kernel(q_ref, k_ref, v_ref, r_ref, g_ref, wg_ref, bg_ref, ng_ref, o_ref, state_ref, *, rows):
    nchunk = rows // GLA_CHUNK

    @pl.when(pl.program_id(1) == 0)
    def _():
        state_ref[...] = jnp.zeros_like(state_ref)

    g_pre = _dot(g_ref[...].astype(BF16), wg_ref[...]) + bg_ref[...]
    log_a = (jnp.minimum(g_pre, 0.0) - jnp.log1p(jnp.exp(-jnp.abs(g_pre)))) / GLA_TAU

    ri = lax.broadcasted_iota(jnp.int32, (rows, rows), 0)
    ci = lax.broadcasted_iota(jnp.int32, (rows, rows), 1)
    same_chunk_causal = (ci <= ri) & ((ri >> GLA_CHUNK_SHIFT) == (ci >> GLA_CHUNK_SHIFT))
    tril = same_chunk_causal.astype(BF16)
    b = jnp.zeros_like(log_a)
    for part in _split3(log_a):
        b = b + _dot(tril, part)
    b_last = jnp.concatenate(
        [jnp.broadcast_to(b[(c + 1) * GLA_CHUNK - 1:(c + 1) * GLA_CHUNK, :], (GLA_CHUNK, b.shape[1]))
         for c in range(nchunk)], axis=0)

    q = q_ref[...] * (GLA_DK ** -0.5)
    k = k_ref[...]
    q_t = (q * jnp.exp(b)).astype(BF16)
    k_t = (k * jnp.exp(-b)).astype(BF16)
    k_end = (k * jnp.exp(b_last - b)).astype(BF16)
    decay = jnp.exp(b_last)
    v = v_ref[...].astype(BF16)
    r = r_ref[...]
    ng = ng_ref[...]

    for h in range(GLA_HEADS):
        ks = slice(h * GLA_DK_PAD, (h + 1) * GLA_DK_PAD)
        vs = slice(h * GLA_DV_PAD, (h + 1) * GLA_DV_PAD)
        attn = jnp.where(same_chunk_causal, _dot_nt(q_t[:, ks], k_t[:, ks]), 0.0).astype(BF16)
        o_intra = _dot(attn, v[:, vs])
        st = state_ref[h]
        o_inter = []
        for c in range(nchunk):
            rs = slice(c * GLA_CHUNK, (c + 1) * GLA_CHUNK)
            o_inter.append(_dot_nt(q_t[rs, ks], st.astype(BF16)))
            st = st * decay[c * GLA_CHUNK:c * GLA_CHUNK + 1, ks] + _dot_tn(v[rs, vs], k_end[rs, ks])
        state_ref[h] = st
        o = o_intra + jnp.concatenate(o_inter, axis=0)
        ms = jnp.sum(o * o, axis=-1, keepdims=True) / GLA_DV
        o = o * lax.rsqrt(ms + HEAD_NORM_EPS) * ng[:, vs]
        rh = r[:, vs]
        o_ref[:, vs] = (rh * jax.nn.sigmoid(rh) * o).astype(o_ref.dtype)


def _gla(q, k, v, r, g, wg, bg, ng, *, batch, seq, rows=256):
    nblk = seq // rows
    row_map = lambda b, i: (b * nblk + i, 0)
    const = lambda b, i: (0, 0)
    qk_w, v_w = GLA_HEADS * GLA_DK_PAD, GLA_HEADS * GLA_DV_PAD
    return pl.pallas_call(
        functools.partial(_gla_kernel, rows=rows),
        grid=(batch, nblk),
        in_specs=[pl.BlockSpec((rows, qk_w), row_map), pl.BlockSpec((rows, qk_w), row_map),
                  pl.BlockSpec((rows, v_w), row_map), pl.BlockSpec((rows, v_w), row_map),
                  pl.BlockSpec((rows, LANES), row_map),
                  pl.BlockSpec((LANES, qk_w), const), pl.BlockSpec((1, qk_w), const),
                  pl.BlockSpec((1, v_w), const)],
        out_specs=pl.BlockSpec((rows, v_w), row_map),
        out_shape=jax.ShapeDtypeStruct((batch * seq, v_w), BF16),
        scratch_shapes=[pltpu.VMEM((GLA_HEADS, GLA_DV_PAD, GLA_DK_PAD), F32)],
        compiler_params=_params(2),
        name="gla",
    )(q, k, v, r, g, wg, bg, ng)


def _dilated_kernel(q_ref, kp_ref, kc_ref, vp_ref, vc_ref, o_ref, lse_ref, *, span, dilation, heads):
    first_key = jnp.where(pl.program_id(1) == 0, DIL_BLOCK, 0)
    qi = lax.broadcasted_iota(jnp.int32, (DIL_BLOCK, 2 * DIL_BLOCK), 0)
    kj = lax.broadcasted_iota(jnp.int32, (DIL_BLOCK, 2 * DIL_BLOCK), 1)
    rel = DIL_BLOCK + qi - kj
    mask = (rel >= 0) & (rel <= span) & (kj >= first_key)
    scale = DIL_HEAD_DIM ** -0.5
    for r in range(dilation):
        rows = pl.ds(r, DIL_BLOCK, stride=dilation) if dilation > 1 else slice(None)
        for h in range(heads):
            hs = slice(h * DIL_HEAD_DIM, (h + 1) * DIL_HEAD_DIM)
            q = q_ref[rows, hs].astype(BF16)
            kcat = jnp.concatenate([kp_ref[rows, hs], kc_ref[rows, hs]], axis=0).astype(BF16)
            vcat = jnp.concatenate([vp_ref[rows, hs], vc_ref[rows, hs]], axis=0).astype(BF16)
            s = jnp.where(mask, _dot_nt(q, kcat) * scale, -jnp.inf)
            m = jnp.max(s, axis=-1, keepdims=True)
            p = jnp.exp(s - m)
            l = jnp.sum(p, axis=-1, keepdims=True)
            o_ref[rows, hs] = _dot(p.astype(BF16), vcat) / l
            lse_ref[rows, hs] = jnp.broadcast_to(m + jnp.log(l), (DIL_BLOCK, DIL_HEAD_DIM))


def _dilated_heads_per_step(dilation):
    return DIL_SLOTS if dilation == 1 else 1


def _dilated_group(q, k, v, *, batch, seq, window, dilation, heads_per_step):
    rows = dilation * DIL_BLOCK
    nb = seq // rows
    cur = lambda b, n, hg: (b * nb + n, hg)
    prev = lambda b, n, hg: (b * nb + jnp.maximum(n - 1, 0), hg)
    blk = (rows, heads_per_step * DIL_HEAD_DIM)
    return pl.pallas_call(
        functools.partial(_dilated_kernel, span=window // dilation, dilation=dilation, heads=heads_per_step),
        grid=(batch, nb, DIL_SLOTS // heads_per_step),
        in_specs=[pl.BlockSpec(blk, cur), pl.BlockSpec(blk, prev), pl.BlockSpec(blk, cur),
                  pl.BlockSpec(blk, prev), pl.BlockSpec(blk, cur)],
        out_specs=[pl.BlockSpec(blk, cur), pl.BlockSpec(blk, cur)],
        out_shape=[jax.ShapeDtypeStruct((batch * seq, DIL_KV_WIDTH), F32)] * 2,
        compiler_params=_params(3),
        name="dilated_attention",
    )(q, k, k, v, v)


def _post_kernel(*refs, n_groups):
    x_ref, qm_ref, km_ref, vm_ref, wmix_ref, wmem_ref, g_ref, b_ref = refs[:8]
    mix_refs = refs[8:-1]
    o_ref = refs[-1]
    if n_groups == 0:
        mix = mix_refs[0][...]
    else:
        lses = [mix_refs[n_groups + g][...] for g in range(n_groups)]
        mx = functools.reduce(jnp.maximum, lses)
        ws = [jnp.exp(l - mx) for l in lses]
        den = functools.reduce(lambda a, c: a + c, ws)
        mix = functools.reduce(lambda a, c: a + c, [w * mix_refs[g][...] for g, w in enumerate(ws)]) / den
    qm = qm_ref[...]
    km = km_ref[...].astype(BF16)
    vm = vm_ref[...].astype(BF16)
    lane = lax.broadcasted_iota(jnp.int32, qm.shape, 1)
    mo = jnp.zeros(qm.shape, F32)
    for h in range(MEM_HEADS):
        head = (lane >> MEM_HEAD_SHIFT) == h
        s = _dot_nt(jnp.where(head, qm, 0.0).astype(BF16), km) * (MEM_HEAD_DIM ** -0.5)
        m = jnp.max(s, axis=-1, keepdims=True)
        p = jnp.exp(s - m)
        p = p / jnp.sum(p, axis=-1, keepdims=True)
        mo = mo + jnp.where(head, _dot(p.astype(BF16), vm), 0.0)
    y = _dot(mix.astype(BF16), wmix_ref[...]) + _dot(mo.astype(BF16), wmem_ref[...])
    o_ref[...] = _layer_norm(DN_ALPHA * x_ref[...] + y, g_ref[...], b_ref[...])


def _post_mixer(x, qm, mkv, w_mix, w_mem, g, b, mixes, *, seq, n_groups, tm=256):
    t = x.shape[0]
    per_batch = seq // tm
    row = lambda i: (i, 0)
    const = lambda i: (0, 0)
    mix_w = w_mix.shape[0]
    in_specs = [pl.BlockSpec((tm, D_MODEL), row), pl.BlockSpec((tm, MEM_WIDTH), row),
                pl.BlockSpec((MEM_LEN, MEM_WIDTH), lambda i: (i // per_batch, 0)),
                pl.BlockSpec((MEM_LEN, MEM_WIDTH), lambda i: (i // per_batch, 1)),
                pl.BlockSpec((mix_w, D_MODEL), const), pl.BlockSpec((MEM_WIDTH, D_MODEL), const),
                pl.BlockSpec((1, D_MODEL), const), pl.BlockSpec((1, D_MODEL), const)]
    in_specs += [pl.BlockSpec((tm, mix_w), row) for _ in mixes]
    return pl.pallas_call(
        functools.partial(_post_kernel, n_groups=n_groups),
        grid=(t // tm,),
        in_specs=in_specs,
        out_specs=pl.BlockSpec((tm, D_MODEL), row),
        out_shape=jax.ShapeDtypeStruct((t, D_MODEL), F32),
        compiler_params=_params(1),
        name="post_mixer",
    )(x, qm, mkv, mkv, w_mix, w_mem, g, b, *mixes)


def _top_rows(s, n_rows, val_ref, idx_ref):
    iota = lax.broadcasted_iota(jnp.int32, s.shape, 0).astype(F32)
    for r in range(PEER_TOPK):
        m = jnp.max(s, axis=0, keepdims=True)
        am = jnp.min(jnp.where(s == m, iota, float(n_rows)), axis=0, keepdims=True)
        val_ref[r:r + 1, :] = m
        idx_ref[r:r + 1, :] = am
        s = jnp.where(iota == am, -jnp.inf, s)


PAIR_COUNTS = tuple(PEER_TOPK // (a + 1) for a in range(PEER_TOPK))
PAIR_OFFSETS = tuple(sum(PAIR_COUNTS[:a]) for a in range(PEER_TOPK))
PAIR_ROWS = -(-sum(PAIR_COUNTS) // 8) * 8


def _route_kernel(x_ref, wq_ref, keys_ref, idx_ref, gate_ref, q_ref, tv_ref, ti_ref, bv_ref, bi_ref,
                  cand_ref, e_ref, gt_ref):
    tm = x_ref.shape[0]
    q_ref[...] = _dot(x_ref[...].astype(BF16), wq_ref[...]).astype(BF16)
    n_pairs = sum(PAIR_COUNTS)
    cand_ref[n_pairs:, :] = jnp.full((PAIR_ROWS - n_pairs, tm), -jnp.inf, F32)
    for h in range(PEER_HEADS):
        for p in range(2):
            c0 = (h * 2 + p) * PEER_HALF
            s = _dot_nt(keys_ref[p], q_ref[:, c0:c0 + PEER_HALF])
            _top_rows(s, PEER_N_KEYS, tv_ref.at[p], ti_ref.at[p])
        for a in range(PEER_TOPK):
            cand_ref[PAIR_OFFSETS[a]:PAIR_OFFSETS[a] + PAIR_COUNTS[a], :] = (
                tv_ref[0, a:a + 1, :] + tv_ref[1, 0:PAIR_COUNTS[a], :])
        _top_rows(cand_ref[...], PAIR_ROWS, bv_ref, bi_ref)
        best_s = bv_ref[...]
        best_r = bi_ref[...].astype(jnp.int32)
        ia = jnp.zeros(best_r.shape, F32)
        ib = jnp.zeros(best_r.shape, F32)
        row0 = jnp.zeros(best_r.shape, jnp.int32)
        for a in range(PEER_TOPK):
            at_least = best_r >= PAIR_OFFSETS[a]
            ia = jnp.where(at_least, ti_ref[0, a:a + 1, :], ia)
            row0 = jnp.where(at_least, PAIR_OFFSETS[a], row0)
        jb = best_r - row0
        for b in range(PEER_TOPK):
            ib = jnp.where(jb == b, ti_ref[1, b:b + 1, :], ib)
        pe = jnp.exp(best_s - best_s[0:1, :])
        rs = slice(h * PEER_TOPK, (h + 1) * PEER_TOPK)
        e_ref[rs, :] = (ia.astype(jnp.int32) * PEER_N_KEYS + ib.astype(jnp.int32)) * PEER_ROW_WORDS
        gt_ref[rs, :] = pe / jnp.sum(pe, axis=0, keepdims=True)
    idx_ref[...] = pltpu.bitcast(pltpu.bitcast(e_ref[...], F32).T, jnp.int32)
    gate_ref[...] = gt_ref[...].T


def _peer_route(x, wq, keys, *, tm=256):
    t = x.shape[0]
    nq = wq.shape[1]
    return pl.pallas_call(
        _route_kernel,
        grid=(t // tm,),
        in_specs=[pl.BlockSpec((tm, D_MODEL), lambda i: (i, 0)),
                  pl.BlockSpec((D_MODEL, nq), lambda i: (0, 0)),
                  pl.BlockSpec((2, PEER_N_KEYS, PEER_HALF), lambda i: (0, 0, 0))],
        out_specs=[pl.BlockSpec((tm, PEER_SEL), lambda i: (i, 0))] * 2,
        out_shape=[jax.ShapeDtypeStruct((t, PEER_SEL), jnp.int32),
                   jax.ShapeDtypeStruct((t, PEER_SEL), F32)],
        scratch_shapes=[pltpu.VMEM((tm, nq), BF16),
                        pltpu.VMEM((2, PEER_TOPK, tm), F32), pltpu.VMEM((2, PEER_TOPK, tm), F32),
                        pltpu.VMEM((PEER_TOPK, tm), F32), pltpu.VMEM((PEER_TOPK, tm), F32),
                        pltpu.VMEM((PAIR_ROWS, tm), F32),
                        pltpu.VMEM((PEER_SEL, tm), jnp.int32), pltpu.VMEM((PEER_SEL, tm), F32)],
        compiler_params=_params(1),
        name="peer_route",
    )(x, wq, keys)


def _pack_kernel(w_ref, o_ref):
    te = w_ref.shape[0]
    for s in range(PEER_ROW_WORDS):
        lo = w_ref[:, 2 * s * LANES:(2 * s + 1) * LANES].astype(BF16).astype(F32)
        hi = w_ref[:, (2 * s + 1) * LANES:(2 * s + 2) * LANES].astype(BF16).astype(F32)
        word = (pltpu.bitcast(lo, jnp.uint32) >> 16) | (pltpu.bitcast(hi, jnp.uint32) & jnp.uint32(0xFFFF0000))
        o_ref[pl.ds(s, te, stride=PEER_ROW_WORDS), :] = word


def _pack_table(w, *, te=512):
    e = w.shape[0]
    return pl.pallas_call(
        _pack_kernel,
        grid=(e // te,),
        in_specs=[pl.BlockSpec((te, D_MODEL), lambda i: (i, 0))],
        out_specs=pl.BlockSpec((te * PEER_ROW_WORDS, LANES), lambda i: (i, 0)),
        out_shape=jax.ShapeDtypeStruct((e * PEER_ROW_WORDS, LANES), jnp.uint32),
        compiler_params=_params(1),
        name="pack_table",
    )(w)


def _expert_block(idx_ref, t, blk, size, tbl_ref):
    rows = []
    for j in range(blk * size, (blk + 1) * size):
        i = pl.multiple_of(idx_ref[t, j], PEER_ROW_WORDS)
        rows.append(tbl_ref[pl.ds(i, PEER_ROW_WORDS), :])
    return pltpu.bitcast(jnp.concatenate(rows, axis=0), BF16)


def _peer_scratch():
    return [pltpu.SMEM((PEER_SUB_TOKENS, PEER_SEL), jnp.int32)] * 2 + [pltpu.SemaphoreType.DMA(())] * 2


def _token_loop(idx_hbm, scratch, compute):
    tbs = PEER_SUB_TOKENS
    idx_a, idx_b, sem_a, sem_b = scratch
    step = pl.program_id(0)
    nxt = jnp.minimum(step + 1, pl.num_programs(0) - 1)

    def copy_a(k):
        return pltpu.make_async_copy(idx_hbm.at[2 * k], idx_a, sem_a)

    def copy_b(k):
        return pltpu.make_async_copy(idx_hbm.at[2 * k + 1], idx_b, sem_b)

    @pl.when(step == 0)
    def _():
        copy_a(0).start()
        copy_a(0).wait()

    copy_b(step).start()
    for g in range(tbs):
        compute(g, idx_a, g)
    copy_a(nxt).start()
    copy_b(step).wait()
    for g in range(tbs):
        compute(tbs + g, idx_b, g)
    copy_a(nxt).wait()


def _peer_u_kernel(idx_hbm, x_ref, gate_ref, tbl_ref, a_ref, hb_ref, x8_ref, *scratch):
    tb = x_ref.shape[0]
    width = PEER_SEL * PEER_CHUNKS
    for c in range(PEER_CHUNKS):
        x8_ref[pl.ds(c, tb, stride=PEER_CHUNKS), :] = x_ref[:, c * LANES:(c + 1) * LANES]
    lane = lax.broadcasted_iota(jnp.int32, (2 * PEER_CHUNKS, width), 1)
    row = lax.broadcasted_iota(jnp.int32, (2 * PEER_CHUNKS, width), 0)
    diag = (lane & (PEER_CHUNKS - 1)) == (row & (PEER_CHUNKS - 1))

    def compute(g, idx_ref, t):
        x8 = x8_ref[g * PEER_CHUNKS:(g + 1) * PEER_CHUNKS, :]
        xh = x8.astype(BF16)
        xl = (x8 - xh.astype(F32)).astype(BF16)
        x16 = jnp.concatenate([xh, xl], axis=0)
        r = jnp.concatenate([_dot_nt(x16, _expert_block(idx_ref, t, blk, PEER_HIDDEN_BLOCK, tbl_ref))
                             for blk in range(PEER_SEL // PEER_HIDDEN_BLOCK)], axis=1)
        hb_ref[g:g + 1, :] = jnp.sum(jnp.where(diag, r, 0.0), axis=0, keepdims=True)

    _token_loop(idx_hbm, scratch, compute)
    gi = lax.broadcasted_iota(jnp.int32, (width, PEER_SEL), 0)
    gj = lax.broadcasted_iota(jnp.int32, (width, PEER_SEL), 1)
    group_sum = ((gi >> PEER_CHUNK_SHIFT) == gj).astype(BF16)
    r = _dot(jnp.concatenate(_split3(hb_ref[...]), axis=0), group_sum)
    h = r[0:tb] + r[tb:2 * tb] + r[2 * tb:3 * tb]
    a_ref[...] = 0.5 * h * (1.0 + lax.erf(h * (2.0 ** -0.5))) * gate_ref[...]


def _peer_hidden(idx, x, gate, tbl):
    t = idx.shape[0]
    tb = 2 * PEER_SUB_TOKENS
    return pl.pallas_call(
        _peer_u_kernel,
        grid=(t // tb,),
        in_specs=[pl.BlockSpec(memory_space=pl.ANY),
                  pl.BlockSpec((tb, D_MODEL), lambda i: (i, 0)),
                  pl.BlockSpec((tb, PEER_SEL), lambda i: (i, 0)),
                  pl.BlockSpec(memory_space=pltpu.VMEM)],
        out_specs=pl.BlockSpec((tb, PEER_SEL), lambda i: (i, 0)),
        out_shape=jax.ShapeDtypeStruct((t, PEER_SEL), F32),
        scratch_shapes=[pltpu.VMEM((tb, PEER_SEL * PEER_CHUNKS), F32),
                        pltpu.VMEM((tb * PEER_CHUNKS, LANES), F32)] + _peer_scratch(),
        compiler_params=_params(1),
        name="peer_hidden",
    )(idx.reshape(t // PEER_SUB_TOKENS, PEER_SUB_TOKENS, PEER_SEL), x, gate, tbl)


def _peer_v_kernel(idx_hbm, a_ref, tbl_ref, f8_ref, ah_ref, al_ref, *scratch):
    tb = a_ref.shape[0]
    width = PEER_SEL * PEER_CHUNKS
    ei = lax.broadcasted_iota(jnp.int32, (PEER_SEL, width), 0)
    ej = lax.broadcasted_iota(jnp.int32, (PEER_SEL, width), 1)
    expand = (ei == (ej >> PEER_CHUNK_SHIFT)).astype(BF16)
    a = a_ref[...]
    ah = a.astype(BF16)
    al = (a - ah.astype(F32)).astype(BF16)
    ahl = _dot(jnp.concatenate([ah, al], axis=0), expand)
    ah_ref[...] = ahl[0:tb]
    al_ref[...] = ahl[tb:2 * tb]
    lane = lax.broadcasted_iota(jnp.int32, (PEER_CHUNKS, width), 1)
    row = lax.broadcasted_iota(jnp.int32, (PEER_CHUNKS, width), 0)
    diag = (lane & (PEER_CHUNKS - 1)) == row

    def compute(g, idx_ref, t):
        a8h = jnp.where(diag, ah_ref[g:g + 1, :], 0.0).astype(BF16)
        a8l = jnp.where(diag, al_ref[g:g + 1, :], 0.0).astype(BF16)
        a16 = jnp.concatenate([a8h, a8l], axis=0)
        lanes = PEER_OUTPUT_BLOCK * PEER_CHUNKS
        r = None
        for blk in range(PEER_SEL // PEER_OUTPUT_BLOCK):
            part = _dot(a16[:, blk * lanes:(blk + 1) * lanes],
                        _expert_block(idx_ref, t, blk, PEER_OUTPUT_BLOCK, tbl_ref))
            r = part if r is None else r + part
        f8_ref[g * PEER_CHUNKS:(g + 1) * PEER_CHUNKS, :] = r[0:PEER_CHUNKS] + r[PEER_CHUNKS:2 * PEER_CHUNKS]

    _token_loop(idx_hbm, scratch, compute)


def _peer_output(idx, a, tbl):
    t = idx.shape[0]
    tb = 2 * PEER_SUB_TOKENS
    return pl.pallas_call(
        _peer_v_kernel,
        grid=(t // tb,),
        in_specs=[pl.BlockSpec(memory_space=pl.ANY),
                  pl.BlockSpec((tb, PEER_SEL), lambda i: (i, 0)),
                  pl.BlockSpec(memory_space=pltpu.VMEM)],
        out_specs=pl.BlockSpec((tb * PEER_CHUNKS, LANES), lambda i: (i, 0)),
        out_shape=jax.ShapeDtypeStruct((t * PEER_CHUNKS, LANES), F32),
        scratch_shapes=[pltpu.VMEM((tb, PEER_SEL * PEER_CHUNKS), F32),
                        pltpu.VMEM((tb, PEER_SEL * PEER_CHUNKS), F32)] + _peer_scratch(),
        compiler_params=_params(1),
        name="peer_output",
    )(idx.reshape(t // PEER_SUB_TOKENS, PEER_SUB_TOKENS, PEER_SEL), a, tbl)


def _add_norm_kernel(x_ref, f8_ref, g_ref, b_ref, o_ref):
    tm = x_ref.shape[0]
    f = jnp.concatenate([f8_ref[pl.ds(c, tm, stride=PEER_CHUNKS), :] for c in range(PEER_CHUNKS)], axis=1)
    o_ref[...] = _layer_norm(DN_ALPHA * x_ref[...] + f, g_ref[...], b_ref[...])


def _add_norm(x, f8, g, b, *, tm=512):
    t = x.shape[0]
    row = lambda i: (i, 0)
    const = lambda i: (0, 0)
    return pl.pallas_call(
        _add_norm_kernel,
        grid=(t // tm,),
        in_specs=[pl.BlockSpec((tm, D_MODEL), row), pl.BlockSpec((tm * PEER_CHUNKS, LANES), row),
                  pl.BlockSpec((1, D_MODEL), const), pl.BlockSpec((1, D_MODEL), const)],
        out_specs=pl.BlockSpec((tm, D_MODEL), row),
        out_shape=jax.ShapeDtypeStruct((t, D_MODEL), F32),
        compiler_params=_params(1),
        name="add_norm",
    )(x, f8, g, b)


def _peer(x, w_q, sub_keys, u, v, g, b):
    idx, gate = _peer_route(x, w_q.astype(BF16), sub_keys.astype(BF16))
    a = _peer_hidden(idx, x, gate, _pack_table(u))
    f8 = _peer_output(idx, a, _pack_table(v))
    return _add_norm(x, f8, g, b)


def _pad_heads(w, heads, dim, dim_pad, axis=-1):
    axis = axis % w.ndim
    shape = w.shape[:axis] + (heads, dim) + w.shape[axis + 1:]
    pad = [(0, 0)] * (w.ndim + 1)
    pad[axis + 1] = (0, dim_pad - dim)
    out = jnp.pad(w.reshape(shape), pad)
    return out.reshape(w.shape[:axis] + (heads * dim_pad,) + w.shape[axis + 1:])


def kernel(x, mem, a_w_in, a_w_gate2, a_b_gate, a_norm_g, b_w_in, shared_w_kv, w_mem_kv, w_out, ln_mix_g,
           ln_mix_b, ln_ffn_g, ln_ffn_b, peer_w_q, peer_sub_keys, peer_u, peer_v):
    batch, seq, d = x.shape
    t = batch * seq
    xt = x.reshape(t, d)
    mem2 = mem.reshape(batch * MEM_LEN, d)
    row = lambda a: a.reshape(1, -1)
    qk_w = GLA_HEADS * GLA_DK
    v_w = GLA_HEADS * GLA_DV
    kd = None
    vd = None
    for l in range(DEPTH):
        (mkv,) = _matmul(mem2, w_mem_kv[l].astype(BF16), [2 * MEM_WIDTH], [F32])
        w_mem = w_out[l, v_w:].astype(BF16)
        if l < N_A_LAYERS:
            w = a_w_in[l]
            q0, k0, v0, r0 = 0, qk_w, 2 * qk_w, 2 * qk_w + v_w
            g0 = r0 + v_w
            m0 = g0 + GLA_GATE_RANK
            w_in = jnp.concatenate([
                _pad_heads(w[:, q0:k0], GLA_HEADS, GLA_DK, GLA_DK_PAD),
                _pad_heads(w[:, k0:v0], GLA_HEADS, GLA_DK, GLA_DK_PAD),
                _pad_heads(w[:, v0:r0], GLA_HEADS, GLA_DV, GLA_DV_PAD),
                _pad_heads(w[:, r0:g0], GLA_HEADS, GLA_DV, GLA_DV_PAD),
                w[:, m0:m0 + MEM_WIDTH],
                jnp.pad(w[:, g0:m0], ((0, 0), (0, LANES - GLA_GATE_RANK)))], axis=1).astype(BF16)
            qkw, vw = GLA_HEADS * GLA_DK_PAD, GLA_HEADS * GLA_DV_PAD
            q, k, v, r, qm, gin = _matmul(xt, w_in, [qkw, qkw, vw, vw, MEM_WIDTH, LANES], [F32] * 6)
            wg = jnp.pad(_pad_heads(a_w_gate2[l], GLA_HEADS, GLA_DK, GLA_DK_PAD),
                         ((0, LANES - GLA_GATE_RANK), (0, 0))).astype(BF16)
            bg = row(_pad_heads(a_b_gate[l], GLA_HEADS, GLA_DK, GLA_DK_PAD))
            ng = row(_pad_heads(a_norm_g[l], GLA_HEADS, GLA_DV, GLA_DV_PAD))
            mix = _gla(q, k, v, r, gin, wg, bg, ng, batch=batch, seq=seq)
            w_mix = _pad_heads(w_out[l, :v_w], GLA_HEADS, GLA_DV, GLA_DV_PAD, axis=0).astype(BF16)
            mixes, n_groups = [mix], 0
        else:
            w_in = b_w_in[l - N_A_LAYERS].astype(BF16)
            n_g = len(DIL_PAIRS)
            outs = _matmul(xt, w_in, [DIL_KV_WIDTH] * n_g + [MEM_WIDTH], [F32] * (n_g + 1))
            qm = outs[n_g]
            res = [_dilated_group(outs[g], kd, vd, batch=batch, seq=seq, window=wd, dilation=dl,
                                  heads_per_step=_dilated_heads_per_step(dl))
                   for g, (wd, dl) in enumerate(DIL_PAIRS)]
            mixes = [o for o, _ in res] + [s for _, s in res]
            n_groups = n_g
            w_mix = w_out[l, :DIL_KV_WIDTH].astype(BF16)
        xt = _post_mixer(xt, qm, mkv, w_mix, w_mem, row(ln_mix_g[l]), row(ln_mix_b[l]), mixes,
                         seq=seq, n_groups=n_groups)
        xt = _peer(xt, peer_w_q[l], peer_sub_keys[l], peer_u[l], peer_v[l], row(ln_ffn_g[l]), row(ln_ffn_b[l]))
        if l == N_A_LAYERS - 1:
            kd, vd = _matmul(xt, shared_w_kv.astype(BF16), [DIL_KV_WIDTH] * 2, [F32] * 2)
    return xt.reshape(batch, seq, d)
```

```python
import functools

import jax
import jax.numpy as jnp
from jax import lax
from jax.experimental import pallas as pl
from jax.experimental.pallas import tpu as pltpu

D_MODEL = 1024
DEPTH = 2
N_A_LAYERS = 1
DN_ALPHA = (2.0 * DEPTH) ** 0.25
LN_EPS = 1e-5
HEAD_NORM_EPS = 1e-6

MEM_LEN = 256
MEM_HEADS = 4
MEM_HEAD_DIM = 64
MEM_HEAD_SHIFT = 6
MEM_WIDTH = 256

GLA_HEADS = 4
GLA_DK = 96
GLA_DV = 192
GLA_DK_PAD = 128
GLA_DV_PAD = 256
GLA_GATE_RANK = 16
GLA_TAU = 16.0
GLA_CHUNK = 64
GLA_CHUNK_SHIFT = 6

DIL_PAIRS = ((128, 1), (512, 4), (2048, 16))
DIL_SLOTS = 6
DIL_HEAD_DIM = 128
DIL_BLOCK = 128
DIL_KV_WIDTH = DIL_SLOTS * DIL_HEAD_DIM

PEER_N_KEYS = 128
PEER_HEADS = 8
PEER_TOPK = 16
PEER_HALF = 128
PEER_SEL = PEER_HEADS * PEER_TOPK
PEER_ROW_WORDS = 4
PEER_CHUNKS = D_MODEL // 128
PEER_CHUNK_SHIFT = 3
PEER_HIDDEN_BLOCK = 32
PEER_OUTPUT_BLOCK = 32
PEER_SUB_TOKENS = 64
PEER_TOPK_SHIFT = 4

LANES = 128
VMEM_LIMIT_BYTES = 56 * 1024 * 1024

BF16 = jnp.bfloat16
F32 = jnp.float32


def _params(n_grid_dims):
    return pltpu.CompilerParams(dimension_semantics=("arbitrary",) * n_grid_dims,
                                vmem_limit_bytes=VMEM_LIMIT_BYTES)


def _dot(a, b):
    return jnp.dot(a, b, preferred_element_type=F32)


def _dot_nt(a, b):
    return lax.dot_general(a, b, (((1,), (1,)), ((), ())), preferred_element_type=F32)


def _dot_tn(a, b):
    return lax.dot_general(a, b, (((0,), (0,)), ((), ())), preferred_element_type=F32)


def _split3(v):
    hi = v.astype(BF16)
    r1 = v - hi.astype(F32)
    mid = r1.astype(BF16)
    lo = (r1 - mid.astype(F32)).astype(BF16)
    return hi, mid, lo


def _layer_norm(z, g, b):
    mu = jnp.mean(z, axis=-1, keepdims=True)
    zc = z - mu
    var = jnp.mean(zc * zc, axis=-1, keepdims=True)
    return zc * lax.rsqrt(var + LN_EPS) * g + b


def _mm_kernel(x_ref, w_ref, *o_refs, splits):
    y = _dot(x_ref[...].astype(BF16), w_ref[...])
    off = 0
    for o_ref, width in zip(o_refs, splits):
        o_ref[...] = y[:, off:off + width].astype(o_ref.dtype)
        off += width


def _matmul(x, w, splits, dtypes, *, tm=256):
    t, k = x.shape
    n = w.shape[1]
    assert sum(splits) == n and t % tm == 0
    return pl.pallas_call(
        functools.partial(_mm_kernel, splits=tuple(splits)),
        grid=(t // tm,),
        in_specs=[pl.BlockSpec((tm, k), lambda i: (i, 0)),
                  pl.BlockSpec((k, n), lambda i: (0, 0))],
        out_specs=[pl.BlockSpec((tm, s), lambda i: (i, 0)) for s in splits],
        out_shape=[jax.ShapeDtypeStruct((t, s), d) for s, d in zip(splits, dtypes)],
        compiler_params=_params(1),
        name="matmul",
    )(x, w)


def _gla_kernel(q_ref, k_ref, v_ref, r_ref, g_ref, wg_ref, bg_ref, ng_ref, o_ref, state_ref, *, rows):
    nchunk = rows // GLA_CHUNK

    @pl.when(pl.program_id(1) == 0)
    def _():
        state_ref[...] = jnp.zeros_like(state_ref)

    g_pre = _dot(g_ref[...].astype(BF16), wg_ref[...]) + bg_ref[...]
    log_a = (jnp.minimum(g_pre, 0.0) - jnp.log1p(jnp.exp(-jnp.abs(g_pre)))) / GLA_TAU

    ri = lax.broadcasted_iota(jnp.int32, (rows, rows), 0)
    ci = lax.broadcasted_iota(jnp.int32, (rows, rows), 1)
    same_chunk_causal = (ci <= ri) & ((ri >> GLA_CHUNK_SHIFT) == (ci >> GLA_CHUNK_SHIFT))
    tril = same_chunk_causal.astype(BF16)
    b = jnp.zeros_like(log_a)
    for part in _split3(log_a):
        b = b + _dot(tril, part)
    b_last = jnp.concatenate(
        [jnp.broadcast_to(b[(c + 1) * GLA_CHUNK - 1:(c + 1) * GLA_CHUNK, :], (GLA_CHUNK, b.shape[1]))
         for c in range(nchunk)], axis=0)

    q = q_ref[...] * (GLA_DK ** -0.5)
    k = k_ref[...]
    q_t = (q * jnp.exp(b)).astype(BF16)
    k_t = (k * jnp.exp(-b)).astype(BF16)
    k_end = (k * jnp.exp(b_last - b)).astype(BF16)
    decay = jnp.exp(b_last)
    v = v_ref[...].astype(BF16)
    r = r_ref[...]
    ng = ng_ref[...]

    for h in range(GLA_HEADS):
        ks = slice(h * GLA_DK_PAD, (h + 1) * GLA_DK_PAD)
        vs = slice(h * GLA_DV_PAD, (h + 1) * GLA_DV_PAD)
        attn = jnp.where(same_chunk_causal, _dot_nt(q_t[:, ks], k_t[:, ks]), 0.0).astype(BF16)
        o_intra = _dot(attn, v[:, vs])
        st = state_ref[h]
        o_inter = []
        for c in range(nchunk):
            rs = slice(c * GLA_CHUNK, (c + 1) * GLA_CHUNK)
            o_inter.append(_dot_nt(q_t[rs, ks], st.astype(BF16)))
            st = st * decay[c * GLA_CHUNK:c * GLA_CHUNK + 1, ks] + _dot_tn(v[rs, vs], k_end[rs, ks])
        state_ref[h] = st
        o = o_intra + jnp.concatenate(o_inter, axis=0)
        ms = jnp.sum(o * o, axis=-1, keepdims=True) / GLA_DV
        o = o * lax.rsqrt(ms + HEAD_NORM_EPS) * ng[:, vs]
        rh = r[:, vs]
        o_ref[:, vs] = (rh * jax.nn.sigmoid(rh) * o).astype(o_ref.dtype)


def _gla(q, k, v, r, g, wg, bg, ng, *, batch, seq, rows=256):
    nblk = seq // rows
    row_map = lambda b, i: (b * nblk + i, 0)
    const = lambda b, i: (0, 0)
    qk_w, v_w = GLA_HEADS * GLA_DK_PAD, GLA_HEADS * GLA_DV_PAD
    return pl.pallas_call(
        functools.partial(_gla_kernel, rows=rows),
        grid=(batch, nblk),
        in_specs=[pl.BlockSpec((rows, qk_w), row_map), pl.BlockSpec((rows, qk_w), row_map),
                  pl.BlockSpec((rows, v_w), row_map), pl.BlockSpec((rows, v_w), row_map),
                  pl.BlockSpec((rows, LANES), row_map),
                  pl.BlockSpec((LANES, qk_w), const), pl.BlockSpec((1, qk_w), const),
                  pl.BlockSpec((1, v_w), const)],
        out_specs=pl.BlockSpec((rows, v_w), row_map),
        out_shape=jax.ShapeDtypeStruct((batch * seq, v_w), BF16),
        scratch_shapes=[pltpu.VMEM((GLA_HEADS, GLA_DV_PAD, GLA_DK_PAD), F32)],
        compiler_params=_params(2),
        name="gla",
    )(q, k, v, r, g, wg, bg, ng)


def _dilated_kernel(q_ref, kp_ref, kc_ref, vp_ref, vc_ref, o_ref, lse_ref, *, span, dilation, heads):
    first_key = jnp.where(pl.program_id(1) == 0, DIL_BLOCK, 0)
    qi = lax.broadcasted_iota(jnp.int32, (DIL_BLOCK, 2 * DIL_BLOCK), 0)
    kj = lax.broadcasted_iota(jnp.int32, (DIL_BLOCK, 2 * DIL_BLOCK), 1)
    rel = DIL_BLOCK + qi - kj
    mask = (rel >= 0) & (rel <= span) & (kj >= first_key)
    scale = DIL_HEAD_DIM ** -0.5
    for r in range(dilation):
        rows = pl.ds(r, DIL_BLOCK, stride=dilation) if dilation > 1 else slice(None)
        for h in range(heads):
            hs = slice(h * DIL_HEAD_DIM, (h + 1) * DIL_HEAD_DIM)
            q = q_ref[rows, hs].astype(BF16)
            kcat = jnp.concatenate([kp_ref[rows, hs], kc_ref[rows, hs]], axis=0).astype(BF16)
            vcat = jnp.concatenate([vp_ref[rows, hs], vc_ref[rows, hs]], axis=0).astype(BF16)
            s = jnp.where(mask, _dot_nt(q, kcat) * scale, -jnp.inf)
            m = jnp.max(s, axis=-1, keepdims=True)
            p = jnp.exp(s - m)
            l = jnp.sum(p, axis=-1, keepdims=True)
            o_ref[rows, hs] = _dot(p.astype(BF16), vcat) / l
            lse_ref[rows, hs] = jnp.broadcast_to(m + jnp.log(l), (DIL_BLOCK, DIL_HEAD_DIM))


def _dilated_heads_per_step(dilation):
    return DIL_SLOTS if dilation == 1 else 1


def _dilated_group(q, k, v, *, batch, seq, window, dilation, heads_per_step):
    rows = dilation * DIL_BLOCK
    nb = seq // rows
    cur = lambda b, n, hg: (b * nb + n, hg)
    prev = lambda b, n, hg: (b * nb + jnp.maximum(n - 1, 0), hg)
    blk = (rows, heads_per_step * DIL_HEAD_DIM)
    return pl.pallas_call(
        functools.partial(_dilated_kernel, span=window // dilation, dilation=dilation, heads=heads_per_step),
        grid=(batch, nb, DIL_SLOTS // heads_per_step),
        in_specs=[pl.BlockSpec(blk, cur), pl.BlockSpec(blk, prev), pl.BlockSpec(blk, cur),
                  pl.BlockSpec(blk, prev), pl.BlockSpec(blk, cur)],
        out_specs=[pl.BlockSpec(blk, cur), pl.BlockSpec(blk, cur)],
        out_shape=[jax.ShapeDtypeStruct((batch * seq, DIL_KV_WIDTH), F32)] * 2,
        compiler_params=_params(3),
        name="dilated_attention",
    )(q, k, k, v, v)


def _post_kernel(*refs, n_groups):
    x_ref, qm_ref, km_ref, vm_ref, wmix_ref, wmem_ref, g_ref, b_ref = refs[:8]
    mix_refs = refs[8:-1]
    o_ref = refs[-1]
    if n_groups == 0:
        mix = mix_refs[0][...]
    else:
        lses = [mix_refs[n_groups + g][...] for g in range(n_groups)]
        mx = functools.reduce(jnp.maximum, lses)
        ws = [jnp.exp(l - mx) for l in lses]
        den = functools.reduce(lambda a, c: a + c, ws)
        mix = functools.reduce(lambda a, c: a + c, [w * mix_refs[g][...] for g, w in enumerate(ws)]) / den
    qm = qm_ref[...]
    km = km_ref[...].astype(BF16)
    vm = vm_ref[...].astype(BF16)
    lane = lax.broadcasted_iota(jnp.int32, qm.shape, 1)
    mo = jnp.zeros(qm.shape, F32)
    for h in range(MEM_HEADS):
        head = (lane >> MEM_HEAD_SHIFT) == h
        s = _dot_nt(jnp.where(head, qm, 0.0).astype(BF16), km) * (MEM_HEAD_DIM ** -0.5)
        m = jnp.max(s, axis=-1, keepdims=True)
        p = jnp.exp(s - m)
        p = p / jnp.sum(p, axis=-1, keepdims=True)
        mo = mo + jnp.where(head, _dot(p.astype(BF16), vm), 0.0)
    y = _dot(mix.astype(BF16), wmix_ref[...]) + _dot(mo.astype(BF16), wmem_ref[...])
    o_ref[...] = _layer_norm(DN_ALPHA * x_ref[...] + y, g_ref[...], b_ref[...])


def _post_mixer(x, qm, mkv, w_mix, w_mem, g, b, mixes, *, seq, n_groups, tm=256):
    t = x.shape[0]
    per_batch = seq // tm
    row = lambda i: (i, 0)
    const = lambda i: (0, 0)
    mix_w = w_mix.shape[0]
    in_specs = [pl.BlockSpec((tm, D_MODEL), row), pl.BlockSpec((tm, MEM_WIDTH), row),
                pl.BlockSpec((MEM_LEN, MEM_WIDTH), lambda i: (i // per_batch, 0)),
                pl.BlockSpec((MEM_LEN, MEM_WIDTH), lambda i: (i // per_batch, 1)),
                pl.BlockSpec((mix_w, D_MODEL), const), pl.BlockSpec((MEM_WIDTH, D_MODEL), const),
                pl.BlockSpec((1, D_MODEL), const), pl.BlockSpec((1, D_MODEL), const)]
    in_specs += [pl.BlockSpec((tm, mix_w), row) for _ in mixes]
    return pl.pallas_call(
        functools.partial(_post_kernel, n_groups=n_groups),
        grid=(t // tm,),
        in_specs=in_specs,
        out_specs=pl.BlockSpec((tm, D_MODEL), row),
        out_shape=jax.ShapeDtypeStruct((t, D_MODEL), F32),
        compiler_params=_params(1),
        name="post_mixer",
    )(x, qm, mkv, mkv, w_mix, w_mem, g, b, *mixes)


def _top_rows(s, n_rows, val_ref, idx_ref):
    iota = lax.broadcasted_iota(jnp.int32, s.shape, 0).astype(F32)
    for r in range(PEER_TOPK):
        m = jnp.max(s, axis=0, keepdims=True)
        am = jnp.min(jnp.where(s == m, iota, float(n_rows)), axis=0, keepdims=True)
        val_ref[r:r + 1, :] = m
        idx_ref[r:r + 1, :] = am
        s = jnp.where(iota == am, -jnp.inf, s)


PAIR_COUNTS = tuple(PEER_TOPK // (a + 1) for a in range(PEER_TOPK))
PAIR_OFFSETS = tuple(sum(PAIR_COUNTS[:a]) for a in range(PEER_TOPK))
PAIR_ROWS = -(-sum(PAIR_COUNTS) // 8) * 8


def _route_kernel(x_ref, wq_ref, keys_ref, idx_ref, gate_ref, q_ref, tv_ref, ti_ref, bv_ref, bi_ref,
                  cand_ref, e_ref, gt_ref):
    tm = x_ref.shape[0]
    q_ref[...] = _dot(x_ref[...].astype(BF16), wq_ref[...]).astype(BF16)
    n_pairs = sum(PAIR_COUNTS)
    cand_ref[n_pairs:, :] = jnp.full((PAIR_ROWS - n_pairs, tm), -jnp.inf, F32)
    for h in range(PEER_HEADS):
        for p in range(2):
            c0 = (h * 2 + p) * PEER_HALF
            s = _dot_nt(keys_ref[p], q_ref[:, c0:c0 + PEER_HALF])
            _top_rows(s, PEER_N_KEYS, tv_ref.at[p], ti_ref.at[p])
        for a in range(PEER_TOPK):
            cand_ref[PAIR_OFFSETS[a]:PAIR_OFFSETS[a] + PAIR_COUNTS[a], :] = (
                tv_ref[0, a:a + 1, :] + tv_ref[1, 0:PAIR_COUNTS[a], :])
        _top_rows(cand_ref[...], PAIR_ROWS, bv_ref, bi_ref)
        best_s = bv_ref[...]
        best_r = bi_ref[...].astype(jnp.int32)
        ia = jnp.zeros(best_r.shape, F32)
        ib = jnp.zeros(best_r.shape, F32)
        row0 = jnp.zeros(best_r.shape, jnp.int32)
        for a in range(PEER_TOPK):
            at_least = best_r >= PAIR_OFFSETS[a]
            ia = jnp.where(at_least, ti_ref[0, a:a + 1, :], ia)
            row0 = jnp.where(at_least, PAIR_OFFSETS[a], row0)
        jb = best_r - row0
        for b in range(PEER_TOPK):
            ib = jnp.where(jb == b, ti_ref[1, b:b + 1, :], ib)
        pe = jnp.exp(best_s - best_s[0:1, :])
        rs = slice(h * PEER_TOPK, (h + 1) * PEER_TOPK)
        e_ref[rs, :] = (ia.astype(jnp.int32) * PEER_N_KEYS + ib.astype(jnp.int32)) * PEER_ROW_WORDS
        gt_ref[rs, :] = pe / jnp.sum(pe, axis=0, keepdims=True)
    idx_ref[...] = pltpu.bitcast(pltpu.bitcast(e_ref[...], F32).T, jnp.int32)
    gate_ref[...] = gt_ref[...].T


def _peer_route(x, wq, keys, *, tm=256):
    t = x.shape[0]
    nq = wq.shape[1]
    return pl.pallas_call(
        _route_kernel,
        grid=(t // tm,),
        in_specs=[pl.BlockSpec((tm, D_MODEL), lambda i: (i, 0)),
                  pl.BlockSpec((D_MODEL, nq), lambda i: (0, 0)),
                  pl.BlockSpec((2, PEER_N_KEYS, PEER_HALF), lambda i: (0, 0, 0))],
        out_specs=[pl.BlockSpec((tm, PEER_SEL), lambda i: (i, 0))] * 2,
        out_shape=[jax.ShapeDtypeStruct((t, PEER_SEL), jnp.int32),
                   jax.ShapeDtypeStruct((t, PEER_SEL), F32)],
        scratch_shapes=[pltpu.VMEM((tm, nq), BF16),
                        pltpu.VMEM((2, PEER_TOPK, tm), F32), pltpu.VMEM((2, PEER_TOPK, tm), F32),
                        pltpu.VMEM((PEER_TOPK, tm), F32), pltpu.VMEM((PEER_TOPK, tm), F32),
                        pltpu.VMEM((PAIR_ROWS, tm), F32),
                        pltpu.VMEM((PEER_SEL, tm), jnp.int32), pltpu.VMEM((PEER_SEL, tm), F32)],
        compiler_params=_params(1),
        name="peer_route",
    )(x, wq, keys)


def _pack_kernel(w_ref, o_ref):
    te = w_ref.shape[0]
    for s in range(PEER_ROW_WORDS):
        lo = w_ref[:, 2 * s * LANES:(2 * s + 1) * LANES].astype(BF16).astype(F32)
        hi = w_ref[:, (2 * s + 1) * LANES:(2 * s + 2) * LANES].astype(BF16).astype(F32)
        word = (pltpu.bitcast(lo, jnp.uint32) >> 16) | (pltpu.bitcast(hi, jnp.uint32) & jnp.uint32(0xFFFF0000))
        o_ref[pl.ds(s, te, stride=PEER_ROW_WORDS), :] = word


def _pack_table(w_all, layer, *, te=512):
    e = w_all.shape[1]
    return pl.pallas_call(
        _pack_kernel,
        grid=(e // te,),
        in_specs=[pl.BlockSpec((None, te, D_MODEL), lambda i: (layer, i, 0))],
        out_specs=pl.BlockSpec((te * PEER_ROW_WORDS, LANES), lambda i: (i, 0)),
        out_shape=jax.ShapeDtypeStruct((e * PEER_ROW_WORDS, LANES), jnp.uint32),
        compiler_params=_params(1),
        name="pack_table",
    )(w_all)


def _expert_block(idx_ref, t, blk, size, tbl_ref):
    rows = []
    for j in range(blk * size, (blk + 1) * size):
        i = pl.multiple_of(idx_ref[t, j], PEER_ROW_WORDS)
        rows.append(tbl_ref[pl.ds(i, PEER_ROW_WORDS), :])
    return pltpu.bitcast(jnp.concatenate(rows, axis=0), BF16)


def _peer_scratch():
    return [pltpu.SMEM((PEER_SUB_TOKENS, PEER_SEL), jnp.int32)] * 2 + [pltpu.SemaphoreType.DMA(())] * 2


def _token_loop(idx_hbm, scratch, compute):
    tbs = PEER_SUB_TOKENS
    idx_a, idx_b, sem_a, sem_b = scratch
    step = pl.program_id(0)
    nxt = jnp.minimum(step + 1, pl.num_programs(0) - 1)

    def copy_a(k):
        return pltpu.make_async_copy(idx_hbm.at[2 * k], idx_a, sem_a)

    def copy_b(k):
        return pltpu.make_async_copy(idx_hbm.at[2 * k + 1], idx_b, sem_b)

    @pl.when(step == 0)
    def _():
        copy_a(0).start()
        copy_a(0).wait()

    copy_b(step).start()
    for g in range(tbs):
        compute(g, idx_a, g)
    copy_a(nxt).start()
    copy_b(step).wait()
    for g in range(tbs):
        compute(tbs + g, idx_b, g)
    copy_a(nxt).wait()


def _peer_u_kernel(idx_hbm, x_ref, gate_ref, tbl_ref, a_ref, hb_ref, x8_ref, *scratch):
    tb = x_ref.shape[0]
    width = PEER_SEL * PEER_CHUNKS
    for c in range(PEER_CHUNKS):
        x8_ref[pl.ds(c, tb, stride=PEER_CHUNKS), :] = x_ref[:, c * LANES:(c + 1) * LANES]
    lane = lax.broadcasted_iota(jnp.int32, (2 * PEER_CHUNKS, width), 1)
    row = lax.broadcasted_iota(jnp.int32, (2 * PEER_CHUNKS, width), 0)
    diag = (lane & (PEER_CHUNKS - 1)) == (row & (PEER_CHUNKS - 1))

    def compute(g, idx_ref, t):
        x8 = x8_ref[g * PEER_CHUNKS:(g + 1) * PEER_CHUNKS, :]
        xh = x8.astype(BF16)
        xl = (x8 - xh.astype(F32)).astype(BF16)
        x16 = jnp.concatenate([xh, xl], axis=0)
        r = jnp.concatenate([_dot_nt(x16, _expert_block(idx_ref, t, blk, PEER_HIDDEN_BLOCK, tbl_ref))
                             for blk in range(PEER_SEL // PEER_HIDDEN_BLOCK)], axis=1)
        hb_ref[g:g + 1, :] = jnp.sum(jnp.where(diag, r, 0.0), axis=0, keepdims=True)

    _token_loop(idx_hbm, scratch, compute)
    gi = lax.broadcasted_iota(jnp.int32, (width, PEER_SEL), 0)
    gj = lax.broadcasted_iota(jnp.int32, (width, PEER_SEL), 1)
    group_sum = ((gi >> PEER_CHUNK_SHIFT) == gj).astype(BF16)
    r = _dot(jnp.concatenate(_split3(hb_ref[...]), axis=0), group_sum)
    h = r[0:tb] + r[tb:2 * tb] + r[2 * tb:3 * tb]
    a_ref[...] = 0.5 * h * (1.0 + lax.erf(h * (2.0 ** -0.5))) * gate_ref[...]


def _peer_hidden(idx, x, gate, tbl):
    t = idx.shape[0]
    tb = 2 * PEER_SUB_TOKENS
    return pl.pallas_call(
        _peer_u_kernel,
        grid=(t // tb,),
        in_specs=[pl.BlockSpec(memory_space=pl.ANY),
                  pl.BlockSpec((tb, D_MODEL), lambda i: (i, 0)),
                  pl.BlockSpec((tb, PEER_SEL), lambda i: (i, 0)),
                  pl.BlockSpec(memory_space=pltpu.VMEM)],
        out_specs=pl.BlockSpec((tb, PEER_SEL), lambda i: (i, 0)),
        out_shape=jax.ShapeDtypeStruct((t, PEER_SEL), F32),
        scratch_shapes=[pltpu.VMEM((tb, PEER_SEL * PEER_CHUNKS), F32),
                        pltpu.VMEM((tb * PEER_CHUNKS, LANES), F32)] + _peer_scratch(),
        compiler_params=_params(1),
        name="peer_hidden",
    )(idx.reshape(t // PEER_SUB_TOKENS, PEER_SUB_TOKENS, PEER_SEL), x, gate, tbl)


def _peer_v_kernel(idx_hbm, a_ref, tbl_ref, x_ref, g_ref, b_ref, o_ref, f8_ref, ah_ref, al_ref, *scratch):
    tb = a_ref.shape[0]
    width = PEER_SEL * PEER_CHUNKS
    ei = lax.broadcasted_iota(jnp.int32, (PEER_SEL, width), 0)
    ej = lax.broadcasted_iota(jnp.int32, (PEER_SEL, width), 1)
    expand = (ei == (ej >> PEER_CHUNK_SHIFT)).astype(BF16)
    a = a_ref[...]
    ah = a.astype(BF16)
    al = (a - ah.astype(F32)).astype(BF16)
    ahl = _dot(jnp.concatenate([ah, al], axis=0), expand)
    ah_ref[...] = ahl[0:tb]
    al_ref[...] = ahl[tb:2 * tb]
    lane = lax.broadcasted_iota(jnp.int32, (PEER_CHUNKS, width), 1)
    row = lax.broadcasted_iota(jnp.int32, (PEER_CHUNKS, width), 0)
    diag = (lane & (PEER_CHUNKS - 1)) == row

    def compute(g, idx_ref, t):
        a8h = jnp.where(diag, ah_ref[g:g + 1, :], 0.0).astype(BF16)
        a8l = jnp.where(diag, al_ref[g:g + 1, :], 0.0).astype(BF16)
        a16 = jnp.concatenate([a8h, a8l], axis=0)
        lanes = PEER_OUTPUT_BLOCK * PEER_CHUNKS
        r = None
        for blk in range(PEER_SEL // PEER_OUTPUT_BLOCK):
            part = _dot(a16[:, blk * lanes:(blk + 1) * lanes],
                        _expert_block(idx_ref, t, blk, PEER_OUTPUT_BLOCK, tbl_ref))
            r = part if r is None else r + part
        f8_ref[g * PEER_CHUNKS:(g + 1) * PEER_CHUNKS, :] = r[0:PEER_CHUNKS] + r[PEER_CHUNKS:2 * PEER_CHUNKS]

    _token_loop(idx_hbm, scratch, compute)
    f = jnp.concatenate([f8_ref[pl.ds(c, tb, stride=PEER_CHUNKS), :] for c in range(PEER_CHUNKS)], axis=1)
    o_ref[...] = _layer_norm(DN_ALPHA * x_ref[...] + f, g_ref[...], b_ref[...])


def _peer_output(idx, a, tbl, x, g, b):
    t = idx.shape[0]
    tb = 2 * PEER_SUB_TOKENS
    const = lambda i: (0, 0)
    return pl.pallas_call(
        _peer_v_kernel,
        grid=(t // tb,),
        in_specs=[pl.BlockSpec(memory_space=pl.ANY),
                  pl.BlockSpec((tb, PEER_SEL), lambda i: (i, 0)),
                  pl.BlockSpec(memory_space=pltpu.VMEM),
                  pl.BlockSpec((tb, D_MODEL), lambda i: (i, 0)),
                  pl.BlockSpec((1, D_MODEL), const), pl.BlockSpec((1, D_MODEL), const)],
        out_specs=pl.BlockSpec((tb, D_MODEL), lambda i: (i, 0)),
        out_shape=jax.ShapeDtypeStruct((t, D_MODEL), F32),
        scratch_shapes=[pltpu.VMEM((tb * PEER_CHUNKS, LANES), F32),
                        pltpu.VMEM((tb, PEER_SEL * PEER_CHUNKS), F32),
                        pltpu.VMEM((tb, PEER_SEL * PEER_CHUNKS), F32)] + _peer_scratch(),
        compiler_params=_params(1),
        name="peer_output",
    )(idx.reshape(t // PEER_SUB_TOKENS, PEER_SUB_TOKENS, PEER_SEL), a, tbl, x, g, b)


def _peer(x, w_q, sub_keys, u_all, v_all, layer, g, b):
    idx, gate = _peer_route(x, w_q.astype(BF16), sub_keys.astype(BF16))
    a = _peer_hidden(idx, x, gate, _pack_table(u_all, layer))
    return _peer_output(idx, a, _pack_table(v_all, layer), x, g, b)


def _pad_heads(w, heads, dim, dim_pad, axis=-1):
    axis = axis % w.ndim
    shape = w.shape[:axis] + (heads, dim) + w.shape[axis + 1:]
    pad = [(0, 0)] * (w.ndim + 1)
    pad[axis + 1] = (0, dim_pad - dim)
    out = jnp.pad(w.reshape(shape), pad)
    return out.reshape(w.shape[:axis] + (heads * dim_pad,) + w.shape[axis + 1:])


def kernel(x, mem, a_w_in, a_w_gate2, a_b_gate, a_norm_g, b_w_in, shared_w_kv, w_mem_kv, w_out, ln_mix_g,
           ln_mix_b, ln_ffn_g, ln_ffn_b, peer_w_q, peer_sub_keys, peer_u, peer_v):
    batch, seq, d = x.shape
    t = batch * seq
    xt = x.reshape(t, d)
    mem2 = mem.reshape(batch * MEM_LEN, d)
    row = lambda a: a.reshape(1, -1)
    qk_w = GLA_HEADS * GLA_DK
    v_w = GLA_HEADS * GLA_DV
    kd = None
    vd = None
    for l in range(DEPTH):
        (mkv,) = _matmul(mem2, w_mem_kv[l].astype(BF16), [2 * MEM_WIDTH], [F32])
        w_mem = w_out[l, v_w:].astype(BF16)
        if l < N_A_LAYERS:
            w = a_w_in[l]
            q0, k0, v0, r0 = 0, qk_w, 2 * qk_w, 2 * qk_w + v_w
            g0 = r0 + v_w
            m0 = g0 + GLA_GATE_RANK
            w_in = jnp.concatenate([
                _pad_heads(w[:, q0:k0], GLA_HEADS, GLA_DK, GLA_DK_PAD),
                _pad_heads(w[:, k0:v0], GLA_HEADS, GLA_DK, GLA_DK_PAD),
                _pad_heads(w[:, v0:r0], GLA_HEADS, GLA_DV, GLA_DV_PAD),
                _pad_heads(w[:, r0:g0], GLA_HEADS, GLA_DV, GLA_DV_PAD),
                w[:, m0:m0 + MEM_WIDTH],
                jnp.pad(w[:, g0:m0], ((0, 0), (0, LANES - GLA_GATE_RANK)))], axis=1).astype(BF16)
            qkw, vw = GLA_HEADS * GLA_DK_PAD, GLA_HEADS * GLA_DV_PAD
            q, k, v, r, qm, gin = _matmul(xt, w_in, [qkw, qkw, vw, vw, MEM_WIDTH, LANES], [F32] * 6)
            wg = jnp.pad(_pad_heads(a_w_gate2[l], GLA_HEADS, GLA_DK, GLA_DK_PAD),
                         ((0, LANES - GLA_GATE_RANK), (0, 0))).astype(BF16)
            bg = row(_pad_heads(a_b_gate[l], GLA_HEADS, GLA_DK, GLA_DK_PAD))
            ng = row(_pad_heads(a_norm_g[l], GLA_HEADS, GLA_DV, GLA_DV_PAD))
            mix = _gla(q, k, v, r, gin, wg, bg, ng, batch=batch, seq=seq)
            w_mix = _pad_heads(w_out[l, :v_w], GLA_HEADS, GLA_DV, GLA_DV_PAD, axis=0).astype(BF16)
            mixes, n_groups = [mix], 0
        else:
            w_in = b_w_in[l - N_A_LAYERS].astype(BF16)
            n_g = len(DIL_PAIRS)
            outs = _matmul(xt, w_in, [DIL_KV_WIDTH] * n_g + [MEM_WIDTH], [F32] * (n_g + 1))
            qm = outs[n_g]
            res = [_dilated_group(outs[g], kd, vd, batch=batch, seq=seq, window=wd, dilation=dl,
                                  heads_per_step=_dilated_heads_per_step(dl))
                   for g, (wd, dl) in enumerate(DIL_PAIRS)]
            mixes = [o for o, _ in res] + [s for _, s in res]
            n_groups = n_g
            w_mix = w_out[l, :DIL_KV_WIDTH].astype(BF16)
        xt = _post_mixer(xt, qm, mkv, w_mix, w_mem, row(ln_mix_g[l]), row(ln_mix_b[l]), mixes,
                         seq=seq, n_groups=n_groups)
        xt = _peer(xt, peer_w_q[l], peer_sub_keys[l], peer_u, peer_v, l, row(ln_ffn_g[l]), row(ln_ffn_b[l]))
        if l == N_A_LAYERS - 1:
            kd, vd = _matmul(xt, shared_w_kv.astype(BF16), [DIL_KV_WIDTH] * 2, [F32] * 2)
    return xt.reshape(batch, seq, d)
```

```python
import functools

import jax
import jax.numpy as jnp
from jax import lax
from jax.experimental import pallas as pl
from jax.experimental.pallas import tpu as pltpu

D_MODEL = 1024
DEPTH = 2
N_A_LAYERS = 1
DN_ALPHA = (2.0 * DEPTH) ** 0.25
LN_EPS = 1e-5
HEAD_NORM_EPS = 1e-6

MEM_LEN = 256
MEM_HEADS = 4
MEM_HEAD_DIM = 64
MEM_HEAD_SHIFT = 6
MEM_WIDTH = 256

GLA_HEADS = 4
GLA_DK = 96
GLA_DV = 192
GLA_DK_PAD = 128
GLA_DV_PAD = 256
GLA_GATE_RANK = 16
GLA_TAU = 16.0
GLA_CHUNK = 64
GLA_CHUNK_SHIFT = 6

DIL_PAIRS = ((128, 1), (512, 4), (2048, 16))
DIL_SLOTS = 6
DIL_HEAD_DIM = 128
DIL_BLOCK = 128
DIL_KV_WIDTH = DIL_SLOTS * DIL_HEAD_DIM

PEER_N_KEYS = 128
PEER_HEADS = 8
PEER_TOPK = 16
PEER_HALF = 128
PEER_SEL = PEER_HEADS * PEER_TOPK
PEER_ROW_WORDS = 4
PEER_CHUNKS = D_MODEL // 128
PEER_CHUNK_SHIFT = 3
PEER_HIDDEN_BLOCK = 32
PEER_OUTPUT_BLOCK = 32
PEER_SUB_TOKENS = 128
PEER_TOPK_SHIFT = 4

LANES = 128
VMEM_LIMIT_BYTES = 56 * 1024 * 1024

BF16 = jnp.bfloat16
F32 = jnp.float32


def _params(n_grid_dims):
    return pltpu.CompilerParams(dimension_semantics=("arbitrary",) * n_grid_dims,
                                vmem_limit_bytes=VMEM_LIMIT_BYTES)


def _dot(a, b):
    return jnp.dot(a, b, preferred_element_type=F32)


def _dot_nt(a, b):
    return lax.dot_general(a, b, (((1,), (1,)), ((), ())), preferred_element_type=F32)


def _dot_tn(a, b):
    return lax.dot_general(a, b, (((0,), (0,)), ((), ())), preferred_element_type=F32)


def _split3(v):
    hi = v.astype(BF16)
    r1 = v - hi.astype(F32)
    mid = r1.astype(BF16)
    lo = (r1 - mid.astype(F32)).astype(BF16)
    return hi, mid, lo


def _layer_norm(z, g, b):
    mu = jnp.mean(z, axis=-1, keepdims=True)
    zc = z - mu
    var = jnp.mean(zc * zc, axis=-1, keepdims=True)
    return zc * lax.rsqrt(var + LN_EPS) * g + b


def _mm_kernel(x_ref, w_ref, *o_refs, splits):
    y = _dot(x_ref[...].astype(BF16), w_ref[...])
    off = 0
    for o_ref, width in zip(o_refs, splits):
        o_ref[...] = y[:, off:off + width].astype(o_ref.dtype)
        off += width


def _matmul(x, w, splits, dtypes, *, tm=256):
    t, k = x.shape
    n = w.shape[1]
    assert sum(splits) == n and t % tm == 0
    return pl.pallas_call(
        functools.partial(_mm_kernel, splits=tuple(splits)),
        grid=(t // tm,),
        in_specs=[pl.BlockSpec((tm, k), lambda i: (i, 0)),
                  pl.BlockSpec((k, n), lambda i: (0, 0))],
        out_specs=[pl.BlockSpec((tm, s), lambda i: (i, 0)) for s in splits],
        out_shape=[jax.ShapeDtypeStruct((t, s), d) for s, d in zip(splits, dtypes)],
        compiler_params=_params(1),
        name="matmul",
    )(x, w)


def _gla_kernel(q_ref, k_ref, v_ref, r_ref, g_ref, wg_ref, bg_ref, ng_ref, o_ref, state_ref, *, rows):
    nchunk = rows // GLA_CHUNK

    @pl.when(pl.program_id(1) == 0)
    def _():
        state_ref[...] = jnp.zeros_like(state_ref)

    g_pre = _dot(g_ref[...].astype(BF16), wg_ref[...]) + bg_ref[...]
    log_a = (jnp.minimum(g_pre, 0.0) - jnp.log1p(jnp.exp(-jnp.abs(g_pre)))) / GLA_TAU

    ri = lax.broadcasted_iota(jnp.int32, (rows, rows), 0)
    ci = lax.broadcasted_iota(jnp.int32, (rows, rows), 1)
    same_chunk_causal = (ci <= ri) & ((ri >> GLA_CHUNK_SHIFT) == (ci >> GLA_CHUNK_SHIFT))
    tril = same_chunk_causal.astype(BF16)
    b = jnp.zeros_like(log_a)
    for part in _split3(log_a):
        b = b + _dot(tril, part)
    b_last = jnp.concatenate(
        [jnp.broadcast_to(b[(c + 1) * GLA_CHUNK - 1:(c + 1) * GLA_CHUNK, :], (GLA_CHUNK, b.shape[1]))
         for c in range(nchunk)], axis=0)

    q = q_ref[...] * (GLA_DK ** -0.5)
    k = k_ref[...]
    q_t = (q * jnp.exp(b)).astype(BF16)
    k_t = (k * jnp.exp(-b)).astype(BF16)
    k_end = (k * jnp.exp(b_last - b)).astype(BF16)
    decay = jnp.exp(b_last)
    v = v_ref[...].astype(BF16)
    r = r_ref[...]
    ng = ng_ref[...]

    for h in range(GLA_HEADS):
        ks = slice(h * GLA_DK_PAD, (h + 1) * GLA_DK_PAD)
        vs = slice(h * GLA_DV_PAD, (h + 1) * GLA_DV_PAD)
        attn = jnp.where(same_chunk_causal, _dot_nt(q_t[:, ks], k_t[:, ks]), 0.0).astype(BF16)
        o_intra = _dot(attn, v[:, vs])
        st = state_ref[h]
        o_inter = []
        for c in range(nchunk):
            rs = slice(c * GLA_CHUNK, (c + 1) * GLA_CHUNK)
            o_inter.append(_dot_nt(q_t[rs, ks], st.astype(BF16)))
            st = st * decay[c * GLA_CHUNK:c * GLA_CHUNK + 1, ks] + _dot_tn(v[rs, vs], k_end[rs, ks])
        state_ref[h] = st
        o = o_intra + jnp.concatenate(o_inter, axis=0)
        ms = jnp.sum(o * o, axis=-1, keepdims=True) / GLA_DV
        o = o * lax.rsqrt(ms + HEAD_NORM_EPS) * ng[:, vs]
        rh = r[:, vs]
        o_ref[:, vs] = (rh * jax.nn.sigmoid(rh) * o).astype(o_ref.dtype)


def _gla(q, k, v, r, g, wg, bg, ng, *, batch, seq, rows=256):
    nblk = seq // rows
    row_map = lambda b, i: (b * nblk + i, 0)
    const = lambda b, i: (0, 0)
    qk_w, v_w = GLA_HEADS * GLA_DK_PAD, GLA_HEADS * GLA_DV_PAD
    return pl.pallas_call(
        functools.partial(_gla_kernel, rows=rows),
        grid=(batch, nblk),
        in_specs=[pl.BlockSpec((rows, qk_w), row_map), pl.BlockSpec((rows, qk_w), row_map),
                  pl.BlockSpec((rows, v_w), row_map), pl.BlockSpec((rows, v_w), row_map),
                  pl.BlockSpec((rows, LANES), row_map),
                  pl.BlockSpec((LANES, qk_w), const), pl.BlockSpec((1, qk_w), const),
                  pl.BlockSpec((1, v_w), const)],
        out_specs=pl.BlockSpec((rows, v_w), row_map),
        out_shape=jax.ShapeDtypeStruct((batch * seq, v_w), BF16),
        scratch_shapes=[pltpu.VMEM((GLA_HEADS, GLA_DV_PAD, GLA_DK_PAD), F32)],
        compiler_params=_params(2),
        name="gla",
    )(q, k, v, r, g, wg, bg, ng)


def _dilated_kernel(q_ref, kp_ref, kc_ref, vp_ref, vc_ref, o_ref, lse_ref, *, span, dilation, heads):
    first_key = jnp.where(pl.program_id(1) == 0, DIL_BLOCK, 0)
    qi = lax.broadcasted_iota(jnp.int32, (DIL_BLOCK, 2 * DIL_BLOCK), 0)
    kj = lax.broadcasted_iota(jnp.int32, (DIL_BLOCK, 2 * DIL_BLOCK), 1)
    rel = DIL_BLOCK + qi - kj
    mask = (rel >= 0) & (rel <= span) & (kj >= first_key)
    scale = DIL_HEAD_DIM ** -0.5
    for r in range(dilation):
        rows = pl.ds(r, DIL_BLOCK, stride=dilation) if dilation > 1 else slice(None)
        for h in range(heads):
            hs = slice(h * DIL_HEAD_DIM, (h + 1) * DIL_HEAD_DIM)
            q = q_ref[rows, hs].astype(BF16)
            kcat = jnp.concatenate([kp_ref[rows, hs], kc_ref[rows, hs]], axis=0).astype(BF16)
            vcat = jnp.concatenate([vp_ref[rows, hs], vc_ref[rows, hs]], axis=0).astype(BF16)
            s = jnp.where(mask, _dot_nt(q, kcat) * scale, -jnp.inf)
            m = jnp.max(s, axis=-1, keepdims=True)
            p = jnp.exp(s - m)
            l = jnp.sum(p, axis=-1, keepdims=True)
            o_ref[rows, hs] = _dot(p.astype(BF16), vcat) / l
            lse_ref[rows, hs] = jnp.broadcast_to(m + jnp.log(l), (DIL_BLOCK, DIL_HEAD_DIM))


def _dilated_heads_per_step(dilation):
    return DIL_SLOTS if dilation == 1 else 1


def _dilated_group(q, k, v, *, batch, seq, window, dilation, heads_per_step):
    rows = dilation * DIL_BLOCK
    nb = seq // rows
    cur = lambda b, n, hg: (b * nb + n, hg)
    prev = lambda b, n, hg: (b * nb + jnp.maximum(n - 1, 0), hg)
    blk = (rows, heads_per_step * DIL_HEAD_DIM)
    return pl.pallas_call(
        functools.partial(_dilated_kernel, span=window // dilation, dilation=dilation, heads=heads_per_step),
        grid=(batch, nb, DIL_SLOTS // heads_per_step),
        in_specs=[pl.BlockSpec(blk, cur), pl.BlockSpec(blk, prev), pl.BlockSpec(blk, cur),
                  pl.BlockSpec(blk, prev), pl.BlockSpec(blk, cur)],
        out_specs=[pl.BlockSpec(blk, cur), pl.BlockSpec(blk, cur)],
        out_shape=[jax.ShapeDtypeStruct((batch * seq, DIL_KV_WIDTH), F32)] * 2,
        compiler_params=_params(3),
        name="dilated_attention",
    )(q, k, k, v, v)


def _post_kernel(*refs, n_groups):
    x_ref, qm_ref, km_ref, vm_ref, wmix_ref, wmem_ref, g_ref, b_ref = refs[:8]
    mix_refs = refs[8:-1]
    o_ref = refs[-1]
    if n_groups == 0:
        mix = mix_refs[0][...]
    else:
        lses = [mix_refs[n_groups + g][...] for g in range(n_groups)]
        mx = functools.reduce(jnp.maximum, lses)
        ws = [jnp.exp(l - mx) for l in lses]
        den = functools.reduce(lambda a, c: a + c, ws)
        mix = functools.reduce(lambda a, c: a + c, [w * mix_refs[g][...] for g, w in enumerate(ws)]) / den
    qm = qm_ref[...]
    km = km_ref[...].astype(BF16)
    vm = vm_ref[...].astype(BF16)
    lane = lax.broadcasted_iota(jnp.int32, qm.shape, 1)
    mo = jnp.zeros(qm.shape, F32)
    for h in range(MEM_HEADS):
        head = (lane >> MEM_HEAD_SHIFT) == h
        s = _dot_nt(jnp.where(head, qm, 0.0).astype(BF16), km) * (MEM_HEAD_DIM ** -0.5)
        m = jnp.max(s, axis=-1, keepdims=True)
        p = jnp.exp(s - m)
        p = p / jnp.sum(p, axis=-1, keepdims=True)
        mo = mo + jnp.where(head, _dot(p.astype(BF16), vm), 0.0)
    y = _dot(mix.astype(BF16), wmix_ref[...]) + _dot(mo.astype(BF16), wmem_ref[...])
    o_ref[...] = _layer_norm(DN_ALPHA * x_ref[...] + y, g_ref[...], b_ref[...])


def _post_mixer(x, qm, mkv, w_mix, w_mem, g, b, mixes, *, seq, n_groups, tm=256):
    t = x.shape[0]
    per_batch = seq // tm
    row = lambda i: (i, 0)
    const = lambda i: (0, 0)
    mix_w = w_mix.shape[0]
    in_specs = [pl.BlockSpec((tm, D_MODEL), row), pl.BlockSpec((tm, MEM_WIDTH), row),
                pl.BlockSpec((MEM_LEN, MEM_WIDTH), lambda i: (i // per_batch, 0)),
                pl.BlockSpec((MEM_LEN, MEM_WIDTH), lambda i: (i // per_batch, 1)),
                pl.BlockSpec((mix_w, D_MODEL), const), pl.BlockSpec((MEM_WIDTH, D_MODEL), const),
                pl.BlockSpec((1, D_MODEL), const), pl.BlockSpec((1, D_MODEL), const)]
    in_specs += [pl.BlockSpec((tm, mix_w), row) for _ in mixes]
    return pl.pallas_call(
        functools.partial(_post_kernel, n_groups=n_groups),
        grid=(t // tm,),
        in_specs=in_specs,
        out_specs=pl.BlockSpec((tm, D_MODEL), row),
        out_shape=jax.ShapeDtypeStruct((t, D_MODEL), F32),
        compiler_params=_params(1),
        name="post_mixer",
    )(x, qm, mkv, mkv, w_mix, w_mem, g, b, *mixes)


def _top_rows(s, n_rows, val_ref, idx_ref):
    iota = lax.broadcasted_iota(jnp.int32, s.shape, 0).astype(F32)
    for r in range(PEER_TOPK):
        m = jnp.max(s, axis=0, keepdims=True)
        am = jnp.min(jnp.where(s == m, iota, float(n_rows)), axis=0, keepdims=True)
        val_ref[r:r + 1, :] = m
        idx_ref[r:r + 1, :] = am
        s = jnp.where(iota == am, -jnp.inf, s)


PAIR_COUNTS = tuple(PEER_TOPK // (a + 1) for a in range(PEER_TOPK))
PAIR_OFFSETS = tuple(sum(PAIR_COUNTS[:a]) for a in range(PEER_TOPK))
PAIR_ROWS = -(-sum(PAIR_COUNTS) // 8) * 8


def _route_kernel(x_ref, wq_ref, keys_ref, idx_ref, gate_ref, q_ref, tv_ref, ti_ref, bv_ref, bi_ref,
                  cand_ref, e_ref, gt_ref):
    tm = x_ref.shape[0]
    q_ref[...] = _dot(x_ref[...].astype(BF16), wq_ref[...]).astype(BF16)
    n_pairs = sum(PAIR_COUNTS)
    cand_ref[n_pairs:, :] = jnp.full((PAIR_ROWS - n_pairs, tm), -jnp.inf, F32)
    for h in range(PEER_HEADS):
        for p in range(2):
            c0 = (h * 2 + p) * PEER_HALF
            s = _dot_nt(keys_ref[p], q_ref[:, c0:c0 + PEER_HALF])
            _top_rows(s, PEER_N_KEYS, tv_ref.at[p], ti_ref.at[p])
        for a in range(PEER_TOPK):
            cand_ref[PAIR_OFFSETS[a]:PAIR_OFFSETS[a] + PAIR_COUNTS[a], :] = (
                tv_ref[0, a:a + 1, :] + tv_ref[1, 0:PAIR_COUNTS[a], :])
        _top_rows(cand_ref[...], PAIR_ROWS, bv_ref, bi_ref)
        best_s = bv_ref[...]
        best_r = bi_ref[...].astype(jnp.int32)
        ia = jnp.zeros(best_r.shape, F32)
        ib = jnp.zeros(best_r.shape, F32)
        row0 = jnp.zeros(best_r.shape, jnp.int32)
        for a in range(PEER_TOPK):
            at_least = best_r >= PAIR_OFFSETS[a]
            ia = jnp.where(at_least, ti_ref[0, a:a + 1, :], ia)
            row0 = jnp.where(at_least, PAIR_OFFSETS[a], row0)
        jb = best_r - row0
        for b in range(PEER_TOPK):
            ib = jnp.where(jb == b, ti_ref[1, b:b + 1, :], ib)
        pe = jnp.exp(best_s - best_s[0:1, :])
        rs = slice(h * PEER_TOPK, (h + 1) * PEER_TOPK)
        e_ref[rs, :] = (ia.astype(jnp.int32) * PEER_N_KEYS + ib.astype(jnp.int32)) * PEER_ROW_WORDS
        gt_ref[rs, :] = pe / jnp.sum(pe, axis=0, keepdims=True)
    idx_ref[...] = pltpu.bitcast(pltpu.bitcast(e_ref[...], F32).T, jnp.int32)
    gate_ref[...] = gt_ref[...].T


def _peer_route(x, wq, keys, *, tm=256):
    t = x.shape[0]
    nq = wq.shape[1]
    return pl.pallas_call(
        _route_kernel,
        grid=(t // tm,),
        in_specs=[pl.BlockSpec((tm, D_MODEL), lambda i: (i, 0)),
                  pl.BlockSpec((D_MODEL, nq), lambda i: (0, 0)),
                  pl.BlockSpec((2, PEER_N_KEYS, PEER_HALF), lambda i: (0, 0, 0))],
        out_specs=[pl.BlockSpec((tm, PEER_SEL), lambda i: (i, 0))] * 2,
        out_shape=[jax.ShapeDtypeStruct((t, PEER_SEL), jnp.int32),
                   jax.ShapeDtypeStruct((t, PEER_SEL), F32)],
        scratch_shapes=[pltpu.VMEM((tm, nq), BF16),
                        pltpu.VMEM((2, PEER_TOPK, tm), F32), pltpu.VMEM((2, PEER_TOPK, tm), F32),
                        pltpu.VMEM((PEER_TOPK, tm), F32), pltpu.VMEM((PEER_TOPK, tm), F32),
                        pltpu.VMEM((PAIR_ROWS, tm), F32),
                        pltpu.VMEM((PEER_SEL, tm), jnp.int32), pltpu.VMEM((PEER_SEL, tm), F32)],
        compiler_params=_params(1),
        name="peer_route",
    )(x, wq, keys)


def _pack_kernel(w_ref, o_ref):
    te = w_ref.shape[0]
    for s in range(PEER_ROW_WORDS):
        lo = w_ref[:, 2 * s * LANES:(2 * s + 1) * LANES].astype(BF16).astype(F32)
        hi = w_ref[:, (2 * s + 1) * LANES:(2 * s + 2) * LANES].astype(BF16).astype(F32)
        word = (pltpu.bitcast(lo, jnp.uint32) >> 16) | (pltpu.bitcast(hi, jnp.uint32) & jnp.uint32(0xFFFF0000))
        o_ref[pl.ds(s, te, stride=PEER_ROW_WORDS), :] = word


def _pack_table(w_all, layer, *, te=512):
    e = w_all.shape[1]
    return pl.pallas_call(
        _pack_kernel,
        grid=(e // te,),
        in_specs=[pl.BlockSpec((None, te, D_MODEL), lambda i: (layer, i, 0))],
        out_specs=pl.BlockSpec((te * PEER_ROW_WORDS, LANES), lambda i: (i, 0)),
        out_shape=jax.ShapeDtypeStruct((e * PEER_ROW_WORDS, LANES), jnp.uint32),
        compiler_params=_params(1),
        name="pack_table",
    )(w_all)


def _expert_block(idx_ref, t, blk, size, tbl_ref):
    rows = []
    for j in range(blk * size, (blk + 1) * size):
        i = pl.multiple_of(idx_ref[t, j], PEER_ROW_WORDS)
        rows.append(tbl_ref[pl.ds(i, PEER_ROW_WORDS), :])
    return pltpu.bitcast(jnp.concatenate(rows, axis=0), BF16)


def _peer_scratch():
    return [pltpu.SMEM((PEER_SUB_TOKENS, PEER_SEL), jnp.int32)] * 2 + [pltpu.SemaphoreType.DMA(())] * 2


def _token_loop(idx_hbm, scratch, compute):
    tbs = PEER_SUB_TOKENS
    idx_a, idx_b, sem_a, sem_b = scratch
    step = pl.program_id(0)
    nxt = jnp.minimum(step + 1, pl.num_programs(0) - 1)

    def copy_a(k):
        return pltpu.make_async_copy(idx_hbm.at[2 * k], idx_a, sem_a)

    def copy_b(k):
        return pltpu.make_async_copy(idx_hbm.at[2 * k + 1], idx_b, sem_b)

    @pl.when(step == 0)
    def _():
        copy_a(0).start()
        copy_a(0).wait()

    copy_b(step).start()
    for g in range(tbs):
        compute(g, idx_a, g)
    copy_a(nxt).start()
    copy_b(step).wait()
    for g in range(tbs):
        compute(tbs + g, idx_b, g)
    copy_a(nxt).wait()


def _peer_u_kernel(idx_hbm, x_ref, gate_ref, tbl_ref, a_ref, hb_ref, x8_ref, *scratch):
    tb = x_ref.shape[0]
    width = PEER_SEL * PEER_CHUNKS
    for c in range(PEER_CHUNKS):
        x8_ref[pl.ds(c, tb, stride=PEER_CHUNKS), :] = x_ref[:, c * LANES:(c + 1) * LANES]
    lane = lax.broadcasted_iota(jnp.int32, (2 * PEER_CHUNKS, width), 1)
    row = lax.broadcasted_iota(jnp.int32, (2 * PEER_CHUNKS, width), 0)
    diag = (lane & (PEER_CHUNKS - 1)) == (row & (PEER_CHUNKS - 1))

    def compute(g, idx_ref, t):
        x8 = x8_ref[g * PEER_CHUNKS:(g + 1) * PEER_CHUNKS, :]
        xh = x8.astype(BF16)
        xl = (x8 - xh.astype(F32)).astype(BF16)
        x16 = jnp.concatenate([xh, xl], axis=0)
        r = jnp.concatenate([_dot_nt(x16, _expert_block(idx_ref, t, blk, PEER_HIDDEN_BLOCK, tbl_ref))
                             for blk in range(PEER_SEL // PEER_HIDDEN_BLOCK)], axis=1)
        hb_ref[g:g + 1, :] = jnp.sum(jnp.where(diag, r, 0.0), axis=0, keepdims=True)

    _token_loop(idx_hbm, scratch, compute)
    gi = lax.broadcasted_iota(jnp.int32, (width, PEER_SEL), 0)
    gj = lax.broadcasted_iota(jnp.int32, (width, PEER_SEL), 1)
    group_sum = ((gi >> PEER_CHUNK_SHIFT) == gj).astype(BF16)
    r = _dot(jnp.concatenate(_split3(hb_ref[...]), axis=0), group_sum)
    h = r[0:tb] + r[tb:2 * tb] + r[2 * tb:3 * tb]
    a_ref[...] = 0.5 * h * (1.0 + lax.erf(h * (2.0 ** -0.5))) * gate_ref[...]


def _peer_hidden(idx, x, gate, tbl):
    t = idx.shape[0]
    tb = 2 * PEER_SUB_TOKENS
    return pl.pallas_call(
        _peer_u_kernel,
        grid=(t // tb,),
        in_specs=[pl.BlockSpec(memory_space=pl.ANY),
                  pl.BlockSpec((tb, D_MODEL), lambda i: (i, 0)),
                  pl.BlockSpec((tb, PEER_SEL), lambda i: (i, 0)),
                  pl.BlockSpec(memory_space=pltpu.VMEM)],
        out_specs=pl.BlockSpec((tb, PEER_SEL), lambda i: (i, 0)),
        out_shape=jax.ShapeDtypeStruct((t, PEER_SEL), F32),
        scratch_shapes=[pltpu.VMEM((tb, PEER_SEL * PEER_CHUNKS), F32),
                        pltpu.VMEM((tb * PEER_CHUNKS, LANES), F32)] + _peer_scratch(),
        compiler_params=_params(1),
        name="peer_hidden",
    )(idx.reshape(t // PEER_SUB_TOKENS, PEER_SUB_TOKENS, PEER_SEL), x, gate, tbl)


def _peer_v_kernel(idx_hbm, a_ref, tbl_ref, x_ref, g_ref, b_ref, o_ref, f8_ref, ah_ref, al_ref, *scratch):
    tb = a_ref.shape[0]
    width = PEER_SEL * PEER_CHUNKS
    ei = lax.broadcasted_iota(jnp.int32, (PEER_SEL, width), 0)
    ej = lax.broadcasted_iota(jnp.int32, (PEER_SEL, width), 1)
    expand = (ei == (ej >> PEER_CHUNK_SHIFT)).astype(BF16)
    a = a_ref[...]
    ah = a.astype(BF16)
    al = (a - ah.astype(F32)).astype(BF16)
    ahl = _dot(jnp.concatenate([ah, al], axis=0), expand)
    ah_ref[...] = ahl[0:tb]
    al_ref[...] = ahl[tb:2 * tb]
    lane = lax.broadcasted_iota(jnp.int32, (PEER_CHUNKS, width), 1)
    row = lax.broadcasted_iota(jnp.int32, (PEER_CHUNKS, width), 0)
    diag = (lane & (PEER_CHUNKS - 1)) == row

    def compute(g, idx_ref, t):
        a8h = jnp.where(diag, ah_ref[g:g + 1, :], 0.0).astype(BF16)
        a8l = jnp.where(diag, al_ref[g:g + 1, :], 0.0).astype(BF16)
        a16 = jnp.concatenate([a8h, a8l], axis=0)
        lanes = PEER_OUTPUT_BLOCK * PEER_CHUNKS
        r = None
        for blk in range(PEER_SEL // PEER_OUTPUT_BLOCK):
            part = _dot(a16[:, blk * lanes:(blk + 1) * lanes],
                        _expert_block(idx_ref, t, blk, PEER_OUTPUT_BLOCK, tbl_ref))
            r = part if r is None else r + part
        f8_ref[g * PEER_CHUNKS:(g + 1) * PEER_CHUNKS, :] = r[0:PEER_CHUNKS] + r[PEER_CHUNKS:2 * PEER_CHUNKS]

    _token_loop(idx_hbm, scratch, compute)
    f = jnp.concatenate([f8_ref[pl.ds(c, tb, stride=PEER_CHUNKS), :] for c in range(PEER_CHUNKS)], axis=1)
    o_ref[...] = _layer_norm(DN_ALPHA * x_ref[...] + f, g_ref[...], b_ref[...])


def _peer_output(idx, a, tbl, x, g, b):
    t = idx.shape[0]
    tb = 2 * PEER_SUB_TOKENS
    const = lambda i: (0, 0)
    return pl.pallas_call(
        _peer_v_kernel,
        grid=(t // tb,),
        in_specs=[pl.BlockSpec(memory_space=pl.ANY),
                  pl.BlockSpec((tb, PEER_SEL), lambda i: (i, 0)),
                  pl.BlockSpec(memory_space=pltpu.VMEM),
                  pl.BlockSpec((tb, D_MODEL), lambda i: (i, 0)),
                  pl.BlockSpec((1, D_MODEL), const), pl.BlockSpec((1, D_MODEL), const)],
        out_specs=pl.BlockSpec((tb, D_MODEL), lambda i: (i, 0)),
        out_shape=jax.ShapeDtypeStruct((t, D_MODEL), F32),
        scratch_shapes=[pltpu.VMEM((tb * PEER_CHUNKS, LANES), F32),
                        pltpu.VMEM((tb, PEER_SEL * PEER_CHUNKS), F32),
                        pltpu.VMEM((tb, PEER_SEL * PEER_CHUNKS), F32)] + _peer_scratch(),
        compiler_params=_params(1),
        name="peer_output",
    )(idx.reshape(t // PEER_SUB_TOKENS, PEER_SUB_TOKENS, PEER_SEL), a, tbl, x, g, b)


def _peer(x, w_q, sub_keys, u_all, v_all, layer, g, b):
    idx, gate = _peer_route(x, w_q.astype(BF16), sub_keys.astype(BF16))
    a = _peer_hidden(idx, x, gate, _pack_table(u_all, layer))
    return _peer_output(idx, a, _pack_table(v_all, layer), x, g, b)


def _pad_heads(w, heads, dim, dim_pad, axis=-1):
    axis = axis % w.ndim
    shape = w.shape[:axis] + (heads, dim) + w.shape[axis + 1:]
    pad = [(0, 0)] * (w.ndim + 1)
    pad[axis + 1] = (0, dim_pad - dim)
    out = jnp.pad(w.reshape(shape), pad)
    return out.reshape(w.shape[:axis] + (heads * dim_pad,) + w.shape[axis + 1:])


def kernel(x, mem, a_w_in, a_w_gate2, a_b_gate, a_norm_g, b_w_in, shared_w_kv, w_mem_kv, w_out, ln_mix_g,
           ln_mix_b, ln_ffn_g, ln_ffn_b, peer_w_q, peer_sub_keys, peer_u, peer_v):
    batch, seq, d = x.shape
    t = batch * seq
    xt = x.reshape(t, d)
    mem2 = mem.reshape(batch * MEM_LEN, d)
    row = lambda a: a.reshape(1, -1)
    qk_w = GLA_HEADS * GLA_DK
    v_w = GLA_HEADS * GLA_DV
    kd = None
    vd = None
    for l in range(DEPTH):
        (mkv,) = _matmul(mem2, w_mem_kv[l].astype(BF16), [2 * MEM_WIDTH], [F32])
        w_mem = w_out[l, v_w:].astype(BF16)
        if l < N_A_LAYERS:
            w = a_w_in[l]
            q0, k0, v0, r0 = 0, qk_w, 2 * qk_w, 2 * qk_w + v_w
            g0 = r0 + v_w
            m0 = g0 + GLA_GATE_RANK
            w_in = jnp.concatenate([
                _pad_heads(w[:, q0:k0], GLA_HEADS, GLA_DK, GLA_DK_PAD),
                _pad_heads(w[:, k0:v0], GLA_HEADS, GLA_DK, GLA_DK_PAD),
                _pad_heads(w[:, v0:r0], GLA_HEADS, GLA_DV, GLA_DV_PAD),
                _pad_heads(w[:, r0:g0], GLA_HEADS, GLA_DV, GLA_DV_PAD),
                w[:, m0:m0 + MEM_WIDTH],
                jnp.pad(w[:, g0:m0], ((0, 0), (0, LANES - GLA_GATE_RANK)))], axis=1).astype(BF16)
            qkw, vw = GLA_HEADS * GLA_DK_PAD, GLA_HEADS * GLA_DV_PAD
            q, k, v, r, qm, gin = _matmul(xt, w_in, [qkw, qkw, vw, vw, MEM_WIDTH, LANES], [F32] * 6)
            wg = jnp.pad(_pad_heads(a_w_gate2[l], GLA_HEADS, GLA_DK, GLA_DK_PAD),
                         ((0, LANES - GLA_GATE_RANK), (0, 0))).astype(BF16)
            bg = row(_pad_heads(a_b_gate[l], GLA_HEADS, GLA_DK, GLA_DK_PAD))
            ng = row(_pad_heads(a_norm_g[l], GLA_HEADS, GLA_DV, GLA_DV_PAD))
            mix = _gla(q, k, v, r, gin, wg, bg, ng, batch=batch, seq=seq)
            w_mix = _pad_heads(w_out[l, :v_w], GLA_HEADS, GLA_DV, GLA_DV_PAD, axis=0).astype(BF16)
            mixes, n_groups = [mix], 0
        else:
            w_in = b_w_in[l - N_A_LAYERS].astype(BF16)
            n_g = len(DIL_PAIRS)
            outs = _matmul(xt, w_in, [DIL_KV_WIDTH] * n_g + [MEM_WIDTH], [F32] * (n_g + 1))
            qm = outs[n_g]
            res = [_dilated_group(outs[g], kd, vd, batch=batch, seq=seq, window=wd, dilation=dl,
                                  heads_per_step=_dilated_heads_per_step(dl))
                   for g, (wd, dl) in enumerate(DIL_PAIRS)]
            mixes = [o for o, _ in res] + [s for _, s in res]
            n_groups = n_g
            w_mix = w_out[l, :DIL_KV_WIDTH].astype(BF16)
        xt = _post_mixer(xt, qm, mkv, w_mix, w_mem, row(ln_mix_g[l]), row(ln_mix_b[l]), mixes,
                         seq=seq, n_groups=n_groups)
        xt = _peer(xt, peer_w_q[l], peer_sub_keys[l], peer_u, peer_v, l, row(ln_ffn_g[l]), row(ln_ffn_b[l]))
        if l == N_A_LAYERS - 1:
            kd, vd = _matmul(xt, shared_w_kv.astype(BF16), [DIL_KV_WIDTH] * 2, [F32] * 2)
    return xt.reshape(batch, seq, d)
```

```python
import functools

import jax
import jax.numpy as jnp
from jax import lax
from jax.experimental import pallas as pl
from jax.experimental.pallas import tpu as pltpu

D_MODEL = 1024
DEPTH = 2
N_A_LAYERS = 1
DN_ALPHA = (2.0 * DEPTH) ** 0.25
LN_EPS = 1e-5
HEAD_NORM_EPS = 1e-6

MEM_LEN = 256
MEM_HEADS = 4
MEM_HEAD_DIM = 64
MEM_HEAD_SHIFT = 6
MEM_WIDTH = 256

GLA_HEADS = 4
GLA_DK = 96
GLA_DV = 192
GLA_DK_PAD = 128
GLA_DV_PAD = 256
GLA_GATE_RANK = 16
GLA_TAU = 16.0
GLA_CHUNK = 64
GLA_CHUNK_SHIFT = 6

DIL_PAIRS = ((128, 1), (512, 4), (2048, 16))
DIL_SLOTS = 6
DIL_HEAD_DIM = 128
DIL_BLOCK = 128
DIL_KV_WIDTH = DIL_SLOTS * DIL_HEAD_DIM

PEER_N_KEYS = 128
PEER_HEADS = 8
PEER_TOPK = 16
PEER_HALF = 128
PEER_SEL = PEER_HEADS * PEER_TOPK
PEER_ROW_WORDS = 4
PEER_CHUNKS = D_MODEL // 128
PEER_CHUNK_SHIFT = 3
PEER_HIDDEN_BLOCK = 32
PEER_OUTPUT_BLOCK = 32
PEER_SUB_TOKENS = 256
PEER_TOPK_SHIFT = 4

LANES = 128
VMEM_LIMIT_BYTES = 56 * 1024 * 1024

BF16 = jnp.bfloat16
F32 = jnp.float32


def _params(n_grid_dims):
    return pltpu.CompilerParams(dimension_semantics=("arbitrary",) * n_grid_dims,
                                vmem_limit_bytes=VMEM_LIMIT_BYTES)


def _dot(a, b):
    return jnp.dot(a, b, preferred_element_type=F32)


def _dot_nt(a, b):
    return lax.dot_general(a, b, (((1,), (1,)), ((), ())), preferred_element_type=F32)


def _dot_tn(a, b):
    return lax.dot_general(a, b, (((0,), (0,)), ((), ())), preferred_element_type=F32)


def _split3(v):
    hi = v.astype(BF16)
    r1 = v - hi.astype(F32)
    mid = r1.astype(BF16)
    lo = (r1 - mid.astype(F32)).astype(BF16)
    return hi, mid, lo


def _layer_norm(z, g, b):
    mu = jnp.mean(z, axis=-1, keepdims=True)
    zc = z - mu
    var = jnp.mean(zc * zc, axis=-1, keepdims=True)
    return zc * lax.rsqrt(var + LN_EPS) * g + b


def _mm_kernel(x_ref, w_ref, *o_refs, splits):
    y = _dot(x_ref[...].astype(BF16), w_ref[...])
    off = 0
    for o_ref, width in zip(o_refs, splits):
        o_ref[...] = y[:, off:off + width].astype(o_ref.dtype)
        off += width


def _matmul(x, w, splits, dtypes, *, tm=256):
    t, k = x.shape
    n = w.shape[1]
    assert sum(splits) == n and t % tm == 0
    return pl.pallas_call(
        functools.partial(_mm_kernel, splits=tuple(splits)),
        grid=(t // tm,),
        in_specs=[pl.BlockSpec((tm, k), lambda i: (i, 0)),
                  pl.BlockSpec((k, n), lambda i: (0, 0))],
        out_specs=[pl.BlockSpec((tm, s), lambda i: (i, 0)) for s in splits],
        out_shape=[jax.ShapeDtypeStruct((t, s), d) for s, d in zip(splits, dtypes)],
        compiler_params=_params(1),
        name="matmul",
    )(x, w)


def _gla_kernel(q_ref, k_ref, v_ref, r_ref, g_ref, wg_ref, bg_ref, ng_ref, o_ref, state_ref, *, rows):
    nchunk = rows // GLA_CHUNK

    @pl.when(pl.program_id(1) == 0)
    def _():
        state_ref[...] = jnp.zeros_like(state_ref)

    g_pre = _dot(g_ref[...].astype(BF16), wg_ref[...]) + bg_ref[...]
    log_a = (jnp.minimum(g_pre, 0.0) - jnp.log1p(jnp.exp(-jnp.abs(g_pre)))) / GLA_TAU

    ri = lax.broadcasted_iota(jnp.int32, (rows, rows), 0)
    ci = lax.broadcasted_iota(jnp.int32, (rows, rows), 1)
    same_chunk_causal = (ci <= ri) & ((ri >> GLA_CHUNK_SHIFT) == (ci >> GLA_CHUNK_SHIFT))
    tril = same_chunk_causal.astype(BF16)
    b = jnp.zeros_like(log_a)
    for part in _split3(log_a):
        b = b + _dot(tril, part)
    b_last = jnp.concatenate(
        [jnp.broadcast_to(b[(c + 1) * GLA_CHUNK - 1:(c + 1) * GLA_CHUNK, :], (GLA_CHUNK, b.shape[1]))
         for c in range(nchunk)], axis=0)

    q = q_ref[...] * (GLA_DK ** -0.5)
    k = k_ref[...]
    q_t = (q * jnp.exp(b)).astype(BF16)
    k_t = (k * jnp.exp(-b)).astype(BF16)
    k_end = (k * jnp.exp(b_last - b)).astype(BF16)
    decay = jnp.exp(b_last)
    v = v_ref[...].astype(BF16)
    r = r_ref[...]
    ng = ng_ref[...]

    for h in range(GLA_HEADS):
        ks = slice(h * GLA_DK_PAD, (h + 1) * GLA_DK_PAD)
        vs = slice(h * GLA_DV_PAD, (h + 1) * GLA_DV_PAD)
        attn = jnp.where(same_chunk_causal, _dot_nt(q_t[:, ks], k_t[:, ks]), 0.0).astype(BF16)
        o_intra = _dot(attn, v[:, vs])
        st = state_ref[h]
        o_inter = []
        for c in range(nchunk):
            rs = slice(c * GLA_CHUNK, (c + 1) * GLA_CHUNK)
            o_inter.append(_dot_nt(q_t[rs, ks], st.astype(BF16)))
            st = st * decay[c * GLA_CHUNK:c * GLA_CHUNK + 1, ks] + _dot_tn(v[rs, vs], k_end[rs, ks])
        state_ref[h] = st
        o = o_intra + jnp.concatenate(o_inter, axis=0)
        ms = jnp.sum(o * o, axis=-1, keepdims=True) / GLA_DV
        o = o * lax.rsqrt(ms + HEAD_NORM_EPS) * ng[:, vs]
        rh = r[:, vs]
        o_ref[:, vs] = (rh * jax.nn.sigmoid(rh) * o).astype(o_ref.dtype)


def _gla(q, k, v, r, g, wg, bg, ng, *, batch, seq, rows=256):
    nblk = seq // rows
    row_map = lambda b, i: (b * nblk + i, 0)
    const = lambda b, i: (0, 0)
    qk_w, v_w = GLA_HEADS * GLA_DK_PAD, GLA_HEADS * GLA_DV_PAD
    return pl.pallas_call(
        functools.partial(_gla_kernel, rows=rows),
        grid=(batch, nblk),
        in_specs=[pl.BlockSpec((rows, qk_w), row_map), pl.BlockSpec((rows, qk_w), row_map),
                  pl.BlockSpec((rows, v_w), row_map), pl.BlockSpec((rows, v_w), row_map),
                  pl.BlockSpec((rows, LANES), row_map),
                  pl.BlockSpec((LANES, qk_w), const), pl.BlockSpec((1, qk_w), const),
                  pl.BlockSpec((1, v_w), const)],
        out_specs=pl.BlockSpec((rows, v_w), row_map),
        out_shape=jax.ShapeDtypeStruct((batch * seq, v_w), BF16),
        scratch_shapes=[pltpu.VMEM((GLA_HEADS, GLA_DV_PAD, GLA_DK_PAD), F32)],
        compiler_params=_params(2),
        name="gla",
    )(q, k, v, r, g, wg, bg, ng)


def _dilated_kernel(q_ref, kp_ref, kc_ref, vp_ref, vc_ref, o_ref, lse_ref, *, span, dilation, heads):
    first_key = jnp.where(pl.program_id(1) == 0, DIL_BLOCK, 0)
    qi = lax.broadcasted_iota(jnp.int32, (DIL_BLOCK, 2 * DIL_BLOCK), 0)
    kj = lax.broadcasted_iota(jnp.int32, (DIL_BLOCK, 2 * DIL_BLOCK), 1)
    rel = DIL_BLOCK + qi - kj
    mask = (rel >= 0) & (rel <= span) & (kj >= first_key)
    scale = DIL_HEAD_DIM ** -0.5
    for r in range(dilation):
        rows = pl.ds(r, DIL_BLOCK, stride=dilation) if dilation > 1 else slice(None)
        for h in range(heads):
            hs = slice(h * DIL_HEAD_DIM, (h + 1) * DIL_HEAD_DIM)
            q = q_ref[rows, hs].astype(BF16)
            kcat = jnp.concatenate([kp_ref[rows, hs], kc_ref[rows, hs]], axis=0).astype(BF16)
            vcat = jnp.concatenate([vp_ref[rows, hs], vc_ref[rows, hs]], axis=0).astype(BF16)
            s = jnp.where(mask, _dot_nt(q, kcat) * scale, -jnp.inf)
            m = jnp.max(s, axis=-1, keepdims=True)
            p = jnp.exp(s - m)
            l = jnp.sum(p, axis=-1, keepdims=True)
            o_ref[rows, hs] = _dot(p.astype(BF16), vcat) / l
            lse_ref[rows, hs] = jnp.broadcast_to(m + jnp.log(l), (DIL_BLOCK, DIL_HEAD_DIM))


def _dilated_heads_per_step(dilation):
    return DIL_SLOTS if dilation == 1 else 1


def _dilated_group(q, k, v, *, batch, seq, window, dilation, heads_per_step):
    rows = dilation * DIL_BLOCK
    nb = seq // rows
    cur = lambda b, n, hg: (b * nb + n, hg)
    prev = lambda b, n, hg: (b * nb + jnp.maximum(n - 1, 0), hg)
    blk = (rows, heads_per_step * DIL_HEAD_DIM)
    return pl.pallas_call(
        functools.partial(_dilated_kernel, span=window // dilation, dilation=dilation, heads=heads_per_step),
        grid=(batch, nb, DIL_SLOTS // heads_per_step),
        in_specs=[pl.BlockSpec(blk, cur), pl.BlockSpec(blk, prev), pl.BlockSpec(blk, cur),
                  pl.BlockSpec(blk, prev), pl.BlockSpec(blk, cur)],
        out_specs=[pl.BlockSpec(blk, cur), pl.BlockSpec(blk, cur)],
        out_shape=[jax.ShapeDtypeStruct((batch * seq, DIL_KV_WIDTH), F32)] * 2,
        compiler_params=_params(3),
        name="dilated_attention",
    )(q, k, k, v, v)


def _post_kernel(*refs, n_groups):
    x_ref, qm_ref, km_ref, vm_ref, wmix_ref, wmem_ref, g_ref, b_ref = refs[:8]
    mix_refs = refs[8:-1]
    o_ref = refs[-1]
    if n_groups == 0:
        mix = mix_refs[0][...]
    else:
        lses = [mix_refs[n_groups + g][...] for g in range(n_groups)]
        mx = functools.reduce(jnp.maximum, lses)
        ws = [jnp.exp(l - mx) for l in lses]
        den = functools.reduce(lambda a, c: a + c, ws)
        mix = functools.reduce(lambda a, c: a + c, [w * mix_refs[g][...] for g, w in enumerate(ws)]) / den
    qm = qm_ref[...]
    km = km_ref[...].astype(BF16)
    vm = vm_ref[...].astype(BF16)
    lane = lax.broadcasted_iota(jnp.int32, qm.shape, 1)
    mo = jnp.zeros(qm.shape, F32)
    for h in range(MEM_HEADS):
        head = (lane >> MEM_HEAD_SHIFT) == h
        s = _dot_nt(jnp.where(head, qm, 0.0).astype(BF16), km) * (MEM_HEAD_DIM ** -0.5)
        m = jnp.max(s, axis=-1, keepdims=True)
        p = jnp.exp(s - m)
        p = p / jnp.sum(p, axis=-1, keepdims=True)
        mo = mo + jnp.where(head, _dot(p.astype(BF16), vm), 0.0)
    y = _dot(mix.astype(BF16), wmix_ref[...]) + _dot(mo.astype(BF16), wmem_ref[...])
    o_ref[...] = _layer_norm(DN_ALPHA * x_ref[...] + y, g_ref[...], b_ref[...])


def _post_mixer(x, qm, mkv, w_mix, w_mem, g, b, mixes, *, seq, n_groups, tm=256):
    t = x.shape[0]
    per_batch = seq // tm
    row = lambda i: (i, 0)
    const = lambda i: (0, 0)
    mix_w = w_mix.shape[0]
    in_specs = [pl.BlockSpec((tm, D_MODEL), row), pl.BlockSpec((tm, MEM_WIDTH), row),
                pl.BlockSpec((MEM_LEN, MEM_WIDTH), lambda i: (i // per_batch, 0)),
                pl.BlockSpec((MEM_LEN, MEM_WIDTH), lambda i: (i // per_batch, 1)),
                pl.BlockSpec((mix_w, D_MODEL), const), pl.BlockSpec((MEM_WIDTH, D_MODEL), const),
                pl.BlockSpec((1, D_MODEL), const), pl.BlockSpec((1, D_MODEL), const)]
    in_specs += [pl.BlockSpec((tm, mix_w), row) for _ in mixes]
    return pl.pallas_call(
        functools.partial(_post_kernel, n_groups=n_groups),
        grid=(t // tm,),
        in_specs=in_specs,
        out_specs=pl.BlockSpec((tm, D_MODEL), row),
        out_shape=jax.ShapeDtypeStruct((t, D_MODEL), F32),
        compiler_params=_params(1),
        name="post_mixer",
    )(x, qm, mkv, mkv, w_mix, w_mem, g, b, *mixes)


def _top_rows(s, n_rows, val_ref, idx_ref):
    iota = lax.broadcasted_iota(jnp.int32, s.shape, 0).astype(F32)
    for r in range(PEER_TOPK):
        m = jnp.max(s, axis=0, keepdims=True)
        am = jnp.min(jnp.where(s == m, iota, float(n_rows)), axis=0, keepdims=True)
        val_ref[r:r + 1, :] = m
        idx_ref[r:r + 1, :] = am
        s = jnp.where(iota == am, -jnp.inf, s)


PAIR_COUNTS = tuple(PEER_TOPK // (a + 1) for a in range(PEER_TOPK))
PAIR_OFFSETS = tuple(sum(PAIR_COUNTS[:a]) for a in range(PEER_TOPK))
PAIR_ROWS = -(-sum(PAIR_COUNTS) // 8) * 8


def _route_kernel(x_ref, wq_ref, keys_ref, idx_ref, gate_ref, q_ref, tv_ref, ti_ref, bv_ref, bi_ref,
                  cand_ref, e_ref, gt_ref):
    tm = x_ref.shape[0]
    q_ref[...] = _dot(x_ref[...].astype(BF16), wq_ref[...]).astype(BF16)
    n_pairs = sum(PAIR_COUNTS)
    cand_ref[n_pairs:, :] = jnp.full((PAIR_ROWS - n_pairs, tm), -jnp.inf, F32)
    for h in range(PEER_HEADS):
        for p in range(2):
            c0 = (h * 2 + p) * PEER_HALF
            s = _dot_nt(keys_ref[p], q_ref[:, c0:c0 + PEER_HALF])
            _top_rows(s, PEER_N_KEYS, tv_ref.at[p], ti_ref.at[p])
        for a in range(PEER_TOPK):
            cand_ref[PAIR_OFFSETS[a]:PAIR_OFFSETS[a] + PAIR_COUNTS[a], :] = (
                tv_ref[0, a:a + 1, :] + tv_ref[1, 0:PAIR_COUNTS[a], :])
        _top_rows(cand_ref[...], PAIR_ROWS, bv_ref, bi_ref)
        best_s = bv_ref[...]
        best_r = bi_ref[...].astype(jnp.int32)
        ia = jnp.zeros(best_r.shape, F32)
        ib = jnp.zeros(best_r.shape, F32)
        row0 = jnp.zeros(best_r.shape, jnp.int32)
        for a in range(PEER_TOPK):
            at_least = best_r >= PAIR_OFFSETS[a]
            ia = jnp.where(at_least, ti_ref[0, a:a + 1, :], ia)
            row0 = jnp.where(at_least, PAIR_OFFSETS[a], row0)
        jb = best_r - row0
        for b in range(PEER_TOPK):
            ib = jnp.where(jb == b, ti_ref[1, b:b + 1, :], ib)
        pe = jnp.exp(best_s - best_s[0:1, :])
        rs = slice(h * PEER_TOPK, (h + 1) * PEER_TOPK)
        e_ref[rs, :] = (ia.astype(jnp.int32) * PEER_N_KEYS + ib.astype(jnp.int32)) * PEER_ROW_WORDS
        gt_ref[rs, :] = pe / jnp.sum(pe, axis=0, keepdims=True)
    idx_ref[...] = pltpu.bitcast(pltpu.bitcast(e_ref[...], F32).T, jnp.int32)
    gate_ref[...] = gt_ref[...].T


def _peer_route(x, wq, keys, *, tm=256):
    t = x.shape[0]
    nq = wq.shape[1]
    return pl.pallas_call(
        _route_kernel,
        grid=(t // tm,),
        in_specs=[pl.BlockSpec((tm, D_MODEL), lambda i: (i, 0)),
                  pl.BlockSpec((D_MODEL, nq), lambda i: (0, 0)),
                  pl.BlockSpec((2, PEER_N_KEYS, PEER_HALF), lambda i: (0, 0, 0))],
        out_specs=[pl.BlockSpec((tm, PEER_SEL), lambda i: (i, 0))] * 2,
        out_shape=[jax.ShapeDtypeStruct((t, PEER_SEL), jnp.int32),
                   jax.ShapeDtypeStruct((t, PEER_SEL), F32)],
        scratch_shapes=[pltpu.VMEM((tm, nq), BF16),
                        pltpu.VMEM((2, PEER_TOPK, tm), F32), pltpu.VMEM((2, PEER_TOPK, tm), F32),
                        pltpu.VMEM((PEER_TOPK, tm), F32), pltpu.VMEM((PEER_TOPK, tm), F32),
                        pltpu.VMEM((PAIR_ROWS, tm), F32),
                        pltpu.VMEM((PEER_SEL, tm), jnp.int32), pltpu.VMEM((PEER_SEL, tm), F32)],
        compiler_params=_params(1),
        name="peer_route",
    )(x, wq, keys)


def _pack_kernel(w_ref, o_ref):
    te = w_ref.shape[0]
    for s in range(PEER_ROW_WORDS):
        lo = w_ref[:, 2 * s * LANES:(2 * s + 1) * LANES].astype(BF16).astype(F32)
        hi = w_ref[:, (2 * s + 1) * LANES:(2 * s + 2) * LANES].astype(BF16).astype(F32)
        word = (pltpu.bitcast(lo, jnp.uint32) >> 16) | (pltpu.bitcast(hi, jnp.uint32) & jnp.uint32(0xFFFF0000))
        o_ref[pl.ds(s, te, stride=PEER_ROW_WORDS), :] = word


def _pack_table(w_all, layer, *, te=512):
    e = w_all.shape[1]
    return pl.pallas_call(
        _pack_kernel,
        grid=(e // te,),
        in_specs=[pl.BlockSpec((None, te, D_MODEL), lambda i: (layer, i, 0))],
        out_specs=pl.BlockSpec((te * PEER_ROW_WORDS, LANES), lambda i: (i, 0)),
        out_shape=jax.ShapeDtypeStruct((e * PEER_ROW_WORDS, LANES), jnp.uint32),
        compiler_params=_params(1),
        name="pack_table",
    )(w_all)


def _expert_block(idx_ref, t, blk, size, tbl_ref):
    rows = []
    for j in range(blk * size, (blk + 1) * size):
        i = pl.multiple_of(idx_ref[t, j], PEER_ROW_WORDS)
        rows.append(tbl_ref[pl.ds(i, PEER_ROW_WORDS), :])
    return pltpu.bitcast(jnp.concatenate(rows, axis=0), BF16)


def _peer_scratch():
    return [pltpu.SMEM((PEER_SUB_TOKENS, PEER_SEL), jnp.int32)] * 2 + [pltpu.SemaphoreType.DMA(())] * 2


def _token_loop(idx_hbm, scratch, compute):
    tbs = PEER_SUB_TOKENS
    idx_a, idx_b, sem_a, sem_b = scratch
    step = pl.program_id(0)
    nxt = jnp.minimum(step + 1, pl.num_programs(0) - 1)

    def copy_a(k):
        return pltpu.make_async_copy(idx_hbm.at[2 * k], idx_a, sem_a)

    def copy_b(k):
        return pltpu.make_async_copy(idx_hbm.at[2 * k + 1], idx_b, sem_b)

    @pl.when(step == 0)
    def _():
        copy_a(0).start()
        copy_a(0).wait()

    copy_b(step).start()
    for g in range(tbs):
        compute(g, idx_a, g)
    copy_a(nxt).start()
    copy_b(step).wait()
    for g in range(tbs):
        compute(tbs + g, idx_b, g)
    copy_a(nxt).wait()


def _peer_u_kernel(idx_hbm, x_ref, gate_ref, tbl_ref, a_ref, hb_ref, x8_ref, *scratch):
    tb = x_ref.shape[0]
    width = PEER_SEL * PEER_CHUNKS
    for c in range(PEER_CHUNKS):
        x8_ref[pl.ds(c, tb, stride=PEER_CHUNKS), :] = x_ref[:, c * LANES:(c + 1) * LANES]
    lane = lax.broadcasted_iota(jnp.int32, (2 * PEER_CHUNKS, width), 1)
    row = lax.broadcasted_iota(jnp.int32, (2 * PEER_CHUNKS, width), 0)
    diag = (lane & (PEER_CHUNKS - 1)) == (row & (PEER_CHUNKS - 1))

    def compute(g, idx_ref, t):
        x8 = x8_ref[g * PEER_CHUNKS:(g + 1) * PEER_CHUNKS, :]
        xh = x8.astype(BF16)
        xl = (x8 - xh.astype(F32)).astype(BF16)
        x16 = jnp.concatenate([xh, xl], axis=0)
        r = jnp.concatenate([_dot_nt(x16, _expert_block(idx_ref, t, blk, PEER_HIDDEN_BLOCK, tbl_ref))
                             for blk in range(PEER_SEL // PEER_HIDDEN_BLOCK)], axis=1)
        hb_ref[g:g + 1, :] = jnp.sum(jnp.where(diag, r, 0.0), axis=0, keepdims=True)

    _token_loop(idx_hbm, scratch, compute)
    gi = lax.broadcasted_iota(jnp.int32, (width, PEER_SEL), 0)
    gj = lax.broadcasted_iota(jnp.int32, (width, PEER_SEL), 1)
    group_sum = ((gi >> PEER_CHUNK_SHIFT) == gj).astype(BF16)
    r = _dot(jnp.concatenate(_split3(hb_ref[...]), axis=0), group_sum)
    h = r[0:tb] + r[tb:2 * tb] + r[2 * tb:3 * tb]
    a_ref[...] = 0.5 * h * (1.0 + lax.erf(h * (2.0 ** -0.5))) * gate_ref[...]


def _peer_hidden(idx, x, gate, tbl):
    t = idx.shape[0]
    tb = 2 * PEER_SUB_TOKENS
    return pl.pallas_call(
        _peer_u_kernel,
        grid=(t // tb,),
        in_specs=[pl.BlockSpec(memory_space=pl.ANY),
                  pl.BlockSpec((tb, D_MODEL), lambda i: (i, 0)),
                  pl.BlockSpec((tb, PEER_SEL), lambda i: (i, 0)),
                  pl.BlockSpec(memory_space=pltpu.VMEM)],
        out_specs=pl.BlockSpec((tb, PEER_SEL), lambda i: (i, 0)),
        out_shape=jax.ShapeDtypeStruct((t, PEER_SEL), F32),
        scratch_shapes=[pltpu.VMEM((tb, PEER_SEL * PEER_CHUNKS), F32),
                        pltpu.VMEM((tb * PEER_CHUNKS, LANES), F32)] + _peer_scratch(),
        compiler_params=_params(1),
        name="peer_hidden",
    )(idx.reshape(t // PEER_SUB_TOKENS, PEER_SUB_TOKENS, PEER_SEL), x, gate, tbl)


def _peer_v_kernel(idx_hbm, a_ref, tbl_ref, x_ref, g_ref, b_ref, o_ref, f8_ref, ah_ref, al_ref, *scratch):
    tb = a_ref.shape[0]
    width = PEER_SEL * PEER_CHUNKS
    ei = lax.broadcasted_iota(jnp.int32, (PEER_SEL, width), 0)
    ej = lax.broadcasted_iota(jnp.int32, (PEER_SEL, width), 1)
    expand = (ei == (ej >> PEER_CHUNK_SHIFT)).astype(BF16)
    a = a_ref[...]
    ah = a.astype(BF16)
    al = (a - ah.astype(F32)).astype(BF16)
    ahl = _dot(jnp.concatenate([ah, al], axis=0), expand)
    ah_ref[...] = ahl[0:tb]
    al_ref[...] = ahl[tb:2 * tb]
    lane = lax.broadcasted_iota(jnp.int32, (PEER_CHUNKS, width), 1)
    row = lax.broadcasted_iota(jnp.int32, (PEER_CHUNKS, width), 0)
    diag = (lane & (PEER_CHUNKS - 1)) == row

    def compute(g, idx_ref, t):
        a8h = jnp.where(diag, ah_ref[g:g + 1, :], 0.0).astype(BF16)
        a8l = jnp.where(diag, al_ref[g:g + 1, :], 0.0).astype(BF16)
        a16 = jnp.concatenate([a8h, a8l], axis=0)
        lanes = PEER_OUTPUT_BLOCK * PEER_CHUNKS
        r = None
        for blk in range(PEER_SEL // PEER_OUTPUT_BLOCK):
            part = _dot(a16[:, blk * lanes:(blk + 1) * lanes],
                        _expert_block(idx_ref, t, blk, PEER_OUTPUT_BLOCK, tbl_ref))
            r = part if r is None else r + part
        f8_ref[g * PEER_CHUNKS:(g + 1) * PEER_CHUNKS, :] = r[0:PEER_CHUNKS] + r[PEER_CHUNKS:2 * PEER_CHUNKS]

    _token_loop(idx_hbm, scratch, compute)
    f = jnp.concatenate([f8_ref[pl.ds(c, tb, stride=PEER_CHUNKS), :] for c in range(PEER_CHUNKS)], axis=1)
    o_ref[...] = _layer_norm(DN_ALPHA * x_ref[...] + f, g_ref[...], b_ref[...])


def _peer_output(idx, a, tbl, x, g, b):
    t = idx.shape[0]
    tb = 2 * PEER_SUB_TOKENS
    const = lambda i: (0, 0)
    return pl.pallas_call(
        _peer_v_kernel,
        grid=(t // tb,),
        in_specs=[pl.BlockSpec(memory_space=pl.ANY),
                  pl.BlockSpec((tb, PEER_SEL), lambda i: (i, 0)),
                  pl.BlockSpec(memory_space=pltpu.VMEM),
                  pl.BlockSpec((tb, D_MODEL), lambda i: (i, 0)),
                  pl.BlockSpec((1, D_MODEL), const), pl.BlockSpec((1, D_MODEL), const)],
        out_specs=pl.BlockSpec((tb, D_MODEL), lambda i: (i, 0)),
        out_shape=jax.ShapeDtypeStruct((t, D_MODEL), F32),
        scratch_shapes=[pltpu.VMEM((tb * PEER_CHUNKS, LANES), F32),
                        pltpu.VMEM((tb, PEER_SEL * PEER_CHUNKS), F32),
                        pltpu.VMEM((tb, PEER_SEL * PEER_CHUNKS), F32)] + _peer_scratch(),
        compiler_params=_params(1),
        name="peer_output",
    )(idx.reshape(t // PEER_SUB_TOKENS, PEER_SUB_TOKENS, PEER_SEL), a, tbl, x, g, b)


def _peer(x, w_q, sub_keys, u_all, v_all, layer, g, b):
    idx, gate = _peer_route(x, w_q.astype(BF16), sub_keys.astype(BF16))
    a = _peer_hidden(idx, x, gate, _pack_table(u_all, layer))
    return _peer_output(idx, a, _pack_table(v_all, layer), x, g, b)


def _pad_heads(w, heads, dim, dim_pad, axis=-1):
    axis = axis % w.ndim
    shape = w.shape[:axis] + (heads, dim) + w.shape[axis + 1:]
    pad = [(0, 0)] * (w.ndim + 1)
    pad[axis + 1] = (0, dim_pad - dim)
    out = jnp.pad(w.reshape(shape), pad)
    return out.reshape(w.shape[:axis] + (heads * dim_pad,) + w.shape[axis + 1:])


def kernel(x, mem, a_w_in, a_w_gate2, a_b_gate, a_norm_g, b_w_in, shared_w_kv, w_mem_kv, w_out, ln_mix_g,
           ln_mix_b, ln_ffn_g, ln_ffn_b, peer_w_q, peer_sub_keys, peer_u, peer_v):
    batch, seq, d = x.shape
    t = batch * seq
    xt = x.reshape(t, d)
    mem2 = mem.reshape(batch * MEM_LEN, d)
    row = lambda a: a.reshape(1, -1)
    qk_w = GLA_HEADS * GLA_DK
    v_w = GLA_HEADS * GLA_DV
    kd = None
    vd = None
    for l in range(DEPTH):
        (mkv,) = _matmul(mem2, w_mem_kv[l].astype(BF16), [2 * MEM_WIDTH], [F32])
        w_mem = w_out[l, v_w:].astype(BF16)
        if l < N_A_LAYERS:
            w = a_w_in[l]
            q0, k0, v0, r0 = 0, qk_w, 2 * qk_w, 2 * qk_w + v_w
            g0 = r0 + v_w
            m0 = g0 + GLA_GATE_RANK
            w_in = jnp.concatenate([
                _pad_heads(w[:, q0:k0], GLA_HEADS, GLA_DK, GLA_DK_PAD),
                _pad_heads(w[:, k0:v0], GLA_HEADS, GLA_DK, GLA_DK_PAD),
                _pad_heads(w[:, v0:r0], GLA_HEADS, GLA_DV, GLA_DV_PAD),
                _pad_heads(w[:, r0:g0], GLA_HEADS, GLA_DV, GLA_DV_PAD),
                w[:, m0:m0 + MEM_WIDTH],
                jnp.pad(w[:, g0:m0], ((0, 0), (0, LANES - GLA_GATE_RANK)))], axis=1).astype(BF16)
            qkw, vw = GLA_HEADS * GLA_DK_PAD, GLA_HEADS * GLA_DV_PAD
            q, k, v, r, qm, gin = _matmul(xt, w_in, [qkw, qkw, vw, vw, MEM_WIDTH, LANES], [F32] * 6)
            wg = jnp.pad(_pad_heads(a_w_gate2[l], GLA_HEADS, GLA_DK, GLA_DK_PAD),
                         ((0, LANES - GLA_GATE_RANK), (0, 0))).astype(BF16)
            bg = row(_pad_heads(a_b_gate[l], GLA_HEADS, GLA_DK, GLA_DK_PAD))
            ng = row(_pad_heads(a_norm_g[l], GLA_HEADS, GLA_DV, GLA_DV_PAD))
            mix = _gla(q, k, v, r, gin, wg, bg, ng, batch=batch, seq=seq)
            w_mix = _pad_heads(w_out[l, :v_w], GLA_HEADS, GLA_DV, GLA_DV_PAD, axis=0).astype(BF16)
            mixes, n_groups = [mix], 0
        else:
            w_in = b_w_in[l - N_A_LAYERS].astype(BF16)
            n_g = len(DIL_PAIRS)
            outs = _matmul(xt, w_in, [DIL_KV_WIDTH] * n_g + [MEM_WIDTH], [F32] * (n_g + 1))
            qm = outs[n_g]
            res = [_dilated_group(outs[g], kd, vd, batch=batch, seq=seq, window=wd, dilation=dl,
                                  heads_per_step=_dilated_heads_per_step(dl))
                   for g, (wd, dl) in enumerate(DIL_PAIRS)]
            mixes = [o for o, _ in res] + [s for _, s in res]
            n_groups = n_g
            w_mix = w_out[l, :DIL_KV_WIDTH].astype(BF16)
        xt = _post_mixer(xt, qm, mkv, w_mix, w_mem, row(ln_mix_g[l]), row(ln_mix_b[l]), mixes,
                         seq=seq, n_groups=n_groups)
        xt = _peer(xt, peer_w_q[l], peer_sub_keys[l], peer_u, peer_v, l, row(ln_ffn_g[l]), row(ln_ffn_b[l]))
        if l == N_A_LAYERS - 1:
            kd, vd = _matmul(xt, shared_w_kv.astype(BF16), [DIL_KV_WIDTH] * 2, [F32] * 2)
    return xt.reshape(batch, seq, d)
```
